```python
import math
import jax, jax.numpy as jnp
from jax import lax
import numpy as np

D_MODEL = 1024
BATCH = 8
SEQ = 8192
DEPTH = 1

CTX_LEN = 256
GRID_W = 64
W_R = 1280
H_R = 5
BW = W_R // H_R
LRU_C = 8.0
CONV_R = 4
CONV_R_LEFT = 2
W_G = 1024
H_G = 8
GC = W_G // H_G
CHUNK = 128
D_FF = 2816
N_MOD = 6
EPS = 1e-6
OFF_RX = 0
OFF_RG = OFF_RX + W_R
OFF_U = OFF_RG + W_R
OFF_V = OFF_U + W_G
OFF_GR = OFF_V + W_G
OFF_GG = OFF_GR + D_MODEL
N_IN = OFF_GG + D_MODEL

kernel_name = "hybrid_rglru_gmlp_convffn_diffusion_block"


def _rmsnorm(x, g):
    xf = x.astype(jnp.float32)
    y = xf * lax.rsqrt(jnp.mean(xf * xf, axis=-1, keepdims=True) + EPS)
    return (y * g.astype(jnp.float32)).astype(x.dtype)


def _modulate(x, g, shift, scale):
    return _rmsnorm(x, g) * (1.0 + scale) + shift


def _dwconv1d(x, w, b):
    C = x.shape[-1]
    y = lax.conv_general_dilated(x, w[:, None, :], window_strides=(1,),
                                 padding=[(CONV_R_LEFT, CONV_R - 1 - CONV_R_LEFT)],
                                 dimension_numbers=('NWC', 'WIO', 'NWC'),
                                 feature_group_count=C)
    return y + b


def _dwconv2d(x, w, b):
    C = x.shape[-1]
    y = lax.conv_general_dilated(x, w[:, :, None, :], window_strides=(1, 1),
                                 padding=[(1, 1), (1, 1)],
                                 dimension_numbers=('NHWC', 'HWIO', 'NHWC'),
                                 feature_group_count=C)
    return y + b


def _linear_scan(a, b, h0, reverse):
    def step(h, ab):
        a_t, b_t = ab
        h = a_t * h + b_t
        return h, h
    h_last, hs = lax.scan(step, h0, (jnp.swapaxes(a, 0, 1), jnp.swapaxes(b, 0, 1)), reverse=reverse)
    return jnp.swapaxes(hs, 0, 1), h_last


def _rglru_dir(xr, lam, wa, ba, wx, bx, h0, reverse):
    B, T, _ = xr.shape
    xh = xr.reshape(B, T, H_R, BW)
    r = jax.nn.sigmoid(jnp.einsum('bthi,hij->bthj', xh, wa).reshape(B, T, W_R) + ba)
    i = jax.nn.sigmoid(jnp.einsum('bthi,hij->bthj', xh, wx).reshape(B, T, W_R) + bx)
    log_a = (-LRU_C * r * jax.nn.softplus(-lam)).astype(jnp.float32)
    a = jnp.exp(log_a)
    mult = jnp.sqrt(-jnp.expm1(2.0 * log_a))
    bt = mult * (i * xr).astype(jnp.float32)
    hs, h_last = _linear_scan(a, bt, h0, reverse)
    return hs.astype(xr.dtype), h_last


def _rglru_bidir(xr, lam, wa, ba, wx, bx, h0_f, h0_b):
    y_f, h_f = _rglru_dir(xr, lam[0], wa[0], ba[0], wx[0], bx[0], h0_f, False)
    y_b, h_b = _rglru_dir(xr, lam[1], wa[1], ba[1], wx[1], bx[1], h0_b, True)
    return y_f + y_b, h_f, h_b


def _chunk_mlp(u_raw, v_raw, g_v, w_s, b_s):
    B, T, _ = u_raw.shape
    u = jax.nn.gelu(u_raw)
    v = _rmsnorm(jax.nn.gelu(v_raw), g_v).reshape(B, T // CHUNK, CHUNK, H_G, GC)
    s = jnp.einsum('hpq,bnqhc->bnphc', w_s, v) + b_s[None, None, :, :, None]
    return u * s.reshape(B, T, W_G)


def _mixer_out(p, y_lru, g_v, w_s, b_s, w_pr, w_pg, w_out):
    y_r = jax.nn.gelu(p[..., OFF_RG:OFF_RG + W_R]) * y_lru
    y_g = _chunk_mlp(p[..., OFF_U:OFF_U + W_G], p[..., OFF_V:OFF_V + W_G], g_v, w_s, b_s)
    gate_r = jax.nn.sigmoid(p[..., OFF_GR:OFF_GR + D_MODEL])
    gate_g = jax.nn.sigmoid(p[..., OFF_GG:OFF_GG + D_MODEL])
    merged = gate_r * (y_r @ w_pr) + gate_g * (y_g @ w_pg)
    return merged @ w_out


def _conv_ffn(h, w_up, cw, cb, w_down, grid_h, grid_w):
    B, T, _ = h.shape
    up = h @ w_up
    g = _dwconv2d(up[..., :D_FF].reshape(B, grid_h, grid_w, D_FF), cw, cb).reshape(B, T, D_FF)
    return (jax.nn.gelu(g) * up[..., D_FF:]) @ w_down


def setup_inputs(seed: int = 0) -> dict:
    key = jax.random.key(seed)
    ks = jax.random.split(key, 32)
    f32 = jnp.float32

    def nrm(k, shape, fan_in, gain=1.0):
        return (gain * fan_in ** -0.5) * jax.random.normal(k, shape, f32)

    u = jax.random.uniform(ks[8], (DEPTH, 2, W_R), f32, minval=0.9, maxval=0.999)
    a1 = u ** (1.0 / LRU_C)
    lru_lam = jnp.log(a1) - jnp.log1p(-a1)
    return {
        "x": jax.random.normal(ks[0], (BATCH, SEQ, D_MODEL), f32),
        "c": jax.random.normal(ks[1], (BATCH, D_MODEL), f32),
        "ctx": jax.random.normal(ks[2], (BATCH, CTX_LEN, D_MODEL), f32),
        "c_ctx": jax.random.normal(ks[3], (D_MODEL,), f32),
        "w_mod": nrm(ks[4], (DEPTH, D_MODEL, N_MOD * D_MODEL), D_MODEL, 0.5),
        "b_mod": 0.02 * jax.random.normal(ks[5], (DEPTH, N_MOD * D_MODEL), f32),
        "g_norm1": 1.0 + 0.02 * jax.random.normal(ks[6], (DEPTH, D_MODEL), f32),
        "w_in": nrm(ks[7], (DEPTH, D_MODEL, N_IN), D_MODEL),
        "conv_w": nrm(ks[9], (DEPTH, CONV_R, W_R), CONV_R),
        "conv_b": 0.02 * jax.random.normal(ks[10], (DEPTH, W_R), f32),
        "lru_lam": lru_lam,
        "lru_wa": nrm(ks[11], (DEPTH, 2, H_R, BW, BW), BW),
        "lru_ba": 0.02 * jax.random.normal(ks[12], (DEPTH, 2, W_R), f32),
        "lru_wx": nrm(ks[13], (DEPTH, 2, H_R, BW, BW), BW),
        "lru_bx": 0.02 * jax.random.normal(ks[14], (DEPTH, 2, W_R), f32),
        "g_v": 1.0 + 0.02 * jax.random.normal(ks[15], (DEPTH, W_G), f32),
        "w_s": nrm(ks[16], (DEPTH, H_G, CHUNK, CHUNK), CHUNK),
        "b_s": 1.0 + 0.02 * jax.random.normal(ks[17], (DEPTH, CHUNK, H_G), f32),
        "w_pr": nrm(ks[18], (DEPTH, W_R, D_MODEL), W_R),
        "w_pg": nrm(ks[19], (DEPTH, W_G, D_MODEL), W_G),
        "w_out": nrm(ks[20], (DEPTH, D_MODEL, D_MODEL), D_MODEL),
        "g_norm2": 1.0 + 0.02 * jax.random.normal(ks[21], (DEPTH, D_MODEL), f32),
        "w_up": nrm(ks[22], (DEPTH, D_MODEL, 2 * D_FF), D_MODEL),
        "ffn_conv_w": nrm(ks[23], (DEPTH, 3, 3, D_FF), 9.0),
        "ffn_conv_b": 0.02 * jax.random.normal(ks[24], (DEPTH, D_FF), f32),
        "w_down": nrm(ks[25], (DEPTH, D_FF, D_MODEL), D_FF),
        "g_final": 1.0 + 0.02 * jax.random.normal(ks[26], (D_MODEL,), f32),
    }


def reference(x, c, ctx, c_ctx, w_mod, b_mod, g_norm1, w_in, conv_w, conv_b, lru_lam,
              lru_wa, lru_ba, lru_wx, lru_bx, g_v, w_s, b_s, w_pr, w_pg, w_out,
              g_norm2, w_up, ffn_conv_w, ffn_conv_b, w_down, g_final):
    B, T, _ = x.shape
    rows = T // GRID_W
    t_ctx = ctx.shape[1]
    for l in range(DEPTH):
        last = l == DEPTH - 1
        m_x = (jax.nn.silu(c) @ w_mod[l] + b_mod[l]).reshape(B, N_MOD, 1, D_MODEL)
        m_c = (jax.nn.silu(c_ctx) @ w_mod[l] + b_mod[l]).reshape(N_MOD, D_MODEL)
        lru = (lru_lam[l], lru_wa[l], lru_ba[l], lru_wx[l], lru_bx[l])
        gmlp_and_merge = (g_v[l], w_s[l], b_s[l], w_pr[l], w_pg[l], w_out[l])

        hc = _modulate(ctx, g_norm1[l], m_c[0], m_c[1])
        pc = hc @ w_in[l][:, :(W_R if last else N_IN)]
        xr_c = _dwconv1d(pc[..., OFF_RX:OFF_RX + W_R], conv_w[l], conv_b[l])
        h0 = jnp.zeros((B, W_R), jnp.float32)
        yc_lru, hc_f, hc_b = _rglru_bidir(xr_c, *lru, h0, h0)

        hx = _modulate(x, g_norm1[l], m_x[:, 0], m_x[:, 1])
        px = hx @ w_in[l]
        xr_x = _dwconv1d(px[..., OFF_RX:OFF_RX + W_R], conv_w[l], conv_b[l])
        yx_lru, _, _ = _rglru_bidir(xr_x, *lru, hc_f, hc_b)
        x = x + m_x[:, 2] * _mixer_out(px, yx_lru, *gmlp_and_merge)
        if not last:
            ctx_mid = ctx + m_c[2] * _mixer_out(pc, yc_lru, *gmlp_and_merge)

        h2 = _modulate(x, g_norm2[l], m_x[:, 3], m_x[:, 4])
        x = x + m_x[:, 5] * _conv_ffn(h2, w_up[l], ffn_conv_w[l], ffn_conv_b[l], w_down[l], rows, GRID_W)
        if not last:
            h2c = _modulate(ctx_mid, g_norm2[l], m_c[3], m_c[4])
            ctx = ctx_mid + m_c[5] * _conv_ffn(h2c, w_up[l], ffn_conv_w[l], ffn_conv_b[l], w_down[l], 1, t_ctx)
    return _rmsnorm(x, g_final)
```

```python
import functools
import math

import jax
import jax.numpy as jnp
from jax import lax
from jax.experimental import pallas as pl
from jax.experimental.pallas import tpu as pltpu

D_MODEL = 1024
GRID_W = 64
W_R = 1280
H_R = 5
BW = W_R // H_R
LRU_C = 8.0
CONV_R = 4
W_G = 1024
H_G = 8
GC = W_G // H_G
CHUNK = 128
D_FF = 2816
N_MOD = 6
EPS = 1e-6
OFF_RG = W_R
OFF_U = OFF_RG + W_R
OFF_V = OFF_U + W_G
OFF_GR = OFF_V + W_G
OFF_GG = OFF_GR + D_MODEL
N_IN = OFF_GG + D_MODEL

V7X_LANES = 128
V7X_SUBLANES = 8
V7X_MXU_WIDTH = 256
V7X_VMEM_BYTES = 64 * 1024 * 1024
BF16_SUBLANE_ROWS = 16

F32 = jnp.float32
BF16 = jnp.bfloat16

LRU_STEPS = 64
MIX_ROWS = 256
FFN_ROWS = 512
FFN_COLS = V7X_MXU_WIDTH
MOD_COLS = 1024


def _vmem_limit(nbytes):
    return int(min(max(2 * nbytes, 16 * 1024 * 1024), V7X_VMEM_BYTES - 8 * 1024 * 1024))


def _gelu(x):
    return 0.5 * x * (1.0 + jnp.tanh(math.sqrt(2.0 / math.pi) * (x + 0.044715 * (x * x * x))))


def _sigmoid(x):
    return 1.0 / (1.0 + jnp.exp(-x))


def _rms(x, g):
    return x * lax.rsqrt(jnp.mean(x * x, axis=-1, keepdims=True) + EPS) * g


def _const_spec(shape):
    nd = len(shape)
    return pl.BlockSpec(shape, lambda *_: (0,) * nd, pipeline_mode=pl.Buffered(1))


def _mod_kernel(c_ref, w_ref, b_ref, o_ref):
    c = c_ref[...]
    s = c * _sigmoid(c)
    o_ref[...] = jnp.dot(s, w_ref[...], preferred_element_type=F32,
                         precision=lax.Precision.HIGHEST) + b_ref[...]


def _mod_call(cc, w, b):
    rows, d = cc.shape
    n = w.shape[1]
    return pl.pallas_call(
        _mod_kernel,
        grid=(n // MOD_COLS,),
        in_specs=[pl.BlockSpec((rows, d), lambda j: (0, 0)),
                  pl.BlockSpec((d, MOD_COLS), lambda j: (0, j)),
                  pl.BlockSpec((1, MOD_COLS), lambda j: (0, j))],
        out_specs=pl.BlockSpec((rows, MOD_COLS), lambda j: (0, j)),
        out_shape=jax.ShapeDtypeStruct((rows, n), F32),
        compiler_params=pltpu.CompilerParams(
            dimension_semantics=("arbitrary",),
            vmem_limit_bytes=_vmem_limit(2 * d * MOD_COLS * 4)),
        name="mod",
    )(cc, w, b)


def _inproj_kernel(x_ref, sh_ref, sc_ref, g_ref, w_ref, o_ref, hs_ref, *, steps):
    nb = x_ref.shape[0]
    g = g_ref[...]
    for b in range(nb):
        hb = _rms(x_ref[b], g) * (1.0 + sc_ref[b:b + 1, :]) + sh_ref[b:b + 1, :]
        for s in range(D_MODEL // V7X_LANES):
            hs_ref[s, pl.ds(b, steps, stride=nb), :] = hb[:, s * V7X_LANES:(s + 1) * V7X_LANES]
    h = jnp.concatenate([hs_ref[s] for s in range(D_MODEL // V7X_LANES)], axis=1).astype(BF16)
    o_ref[...] = jnp.dot(h, w_ref[...], preferred_element_type=F32).astype(BF16)


def _inproj_call(x, shift, scale, g, w_r):
    nb, t, d = x.shape
    steps = LRU_STEPS
    rows = steps * nb
    est = 2 * nb * steps * d * 4 + d * W_R * 2 + 2 * rows * W_R * 2 + rows * d * 4
    return pl.pallas_call(
        functools.partial(_inproj_kernel, steps=steps),
        grid=(t // steps,),
        in_specs=[pl.BlockSpec((nb, steps, d), lambda i: (0, i, 0)),
                  _const_spec((nb, d)), _const_spec((nb, d)), _const_spec((1, d)),
                  _const_spec((d, W_R))],
        out_specs=pl.BlockSpec((rows, W_R), lambda i: (i, 0)),
        out_shape=jax.ShapeDtypeStruct((t * nb, W_R), BF16),
        scratch_shapes=[pltpu.VMEM((d // V7X_LANES, rows, V7X_LANES), F32)],
        compiler_params=pltpu.CompilerParams(
            dimension_semantics=("arbitrary",), vmem_limit_bytes=_vmem_limit(est)),
        name="inproj",
    )(x, shift, scale, g, w_r)


def _lru_kernel(pfm, pfp, pfn, pbm, pbp, pbn, cw_ref, cb_ref, lam_ref, wg_ref, bg_ref, h0_ref,
                yf_ref, yb_ref, hl_ref, a_s, b_s, y_s, hc, *, steps, n_tiles):
    i = pl.program_id(0)
    nb = V7X_SUBLANES
    rows = steps * nb
    n_slab = W_R // V7X_LANES

    @pl.when(i == 0)
    def _():
        hc[...] = h0_ref[...]

    z = -lam_ref[...]
    softplus = jnp.maximum(z, 0.0) + jnp.log1p(jnp.exp(-jnp.abs(z)))
    cw = cw_ref[...]
    cb = cb_ref[...]

    def coefficients(d, tile, pm, pp, pn):
        keep_prev = jnp.where(tile > 0, 1.0, 0.0)
        keep_next = jnp.where(tile < n_tiles - 1, 1.0, 0.0)
        prev2 = pp[...].astype(F32) * keep_prev
        next1 = pn[...].astype(F32)[0:nb] * keep_next
        pe = jnp.concatenate([prev2, pm[...].astype(F32), next1], axis=0)
        xr = cb
        for k in range(CONV_R):
            xr = xr + cw[k:k + 1, :] * pe[k * nb:k * nb + rows]
        xrb = xr.astype(BF16)
        for h in range(H_R):
            cs = slice(h * BW, (h + 1) * BW)
            gates = jnp.dot(xrb[:, cs], wg_ref[d, h], preferred_element_type=F32)
            r = _sigmoid(gates[:, :BW] + bg_ref[2 * d:2 * d + 1, cs])
            ig = _sigmoid(gates[:, BW:] + bg_ref[2 * d + 1:2 * d + 2, cs])
            a = jnp.exp((-LRU_C * r) * softplus[d:d + 1, cs])
            a_s[d, :, cs] = a
            b_s[d, :, cs] = jnp.sqrt(1.0 - a * a) * (ig * xr[:, cs])

    coefficients(0, i, pfm, pfp, pfn)
    coefficients(1, n_tiles - 1 - i, pbm, pbp, pbn)

    def step(t, carry):
        hf, hb = carry
        rf = pl.ds(pl.multiple_of(t * nb, nb), nb)
        rb = pl.ds(pl.multiple_of((steps - 1 - t) * nb, nb), nb)
        hf = a_s[0, rf, :] * hf + b_s[0, rf, :]
        hb = a_s[1, rb, :] * hb + b_s[1, rb, :]
        for s in range(n_slab):
            y_s[0, s, rf, :] = hf[:, s * V7X_LANES:(s + 1) * V7X_LANES]
            y_s[1, s, rb, :] = hb[:, s * V7X_LANES:(s + 1) * V7X_LANES]
        return hf, hb

    hf, hb = lax.fori_loop(0, steps, step, (hc[0], hc[1]), unroll=4)
    hc[0] = hf
    hc[1] = hb
    hl_ref[0] = hf
    hl_ref[1] = hb

    for b in range(nb):
        yf_ref[b] = jnp.concatenate(
            [y_s[0, s, pl.ds(b, steps, stride=nb), :] for s in range(n_slab)], axis=1).astype(BF16)
        yb_ref[b] = jnp.concatenate(
            [y_s[1, s, pl.ds(b, steps, stride=nb), :] for s in range(n_slab)], axis=1).astype(BF16)


def _lru_call(p, nb, cw, cb, lam, wg, bg, h0):
    steps = LRU_STEPS
    rows = steps * nb
    t = p.shape[0] // nb
    n_tiles = t // steps
    halo = BF16_SUBLANE_ROWS
    r_h = rows // halo
    n_h = p.shape[0] // halo

    def main(tile):
        return pl.BlockSpec((rows, W_R), lambda i: (tile(i), 0))

    def prev(tile):
        return pl.BlockSpec((halo, W_R), lambda i: (jnp.maximum(tile(i) * r_h - 1, 0), 0))

    def nxt(tile):
        return pl.BlockSpec((halo, W_R), lambda i: (jnp.minimum((tile(i) + 1) * r_h, n_h - 1), 0))

    fwd = lambda i: i
    bwd = lambda i: n_tiles - 1 - i
    est = (2 * 2 * rows * W_R * 2 + 2 * 2 * nb * steps * W_R * 2 + 3 * 2 * rows * W_R * 4
           + 2 * H_R * BW * 2 * BW * 2 + 4 * rows * W_R * 4)
    return pl.pallas_call(
        functools.partial(_lru_kernel, steps=steps, n_tiles=n_tiles),
        grid=(n_tiles,),
        in_specs=[main(fwd), prev(fwd), nxt(fwd), main(bwd), prev(bwd), nxt(bwd),
                  _const_spec((CONV_R, W_R)), _const_spec((1, W_R)), _const_spec((2, W_R)),
                  _const_spec((2, H_R, BW, 2 * BW)), _const_spec((4, W_R)),
                  _const_spec((2, nb, W_R))],
        out_specs=[pl.BlockSpec((nb, steps, W_R), lambda i: (0, i, 0)),
                   pl.BlockSpec((nb, steps, W_R), lambda i: (0, n_tiles - 1 - i, 0)),
                   pl.BlockSpec((2, nb, W_R), lambda i: (0, 0, 0))],
        out_shape=[jax.ShapeDtypeStruct((nb, t, W_R), BF16),
                   jax.ShapeDtypeStruct((nb, t, W_R), BF16),
                   jax.ShapeDtypeStruct((2, nb, W_R), F32)],
        scratch_shapes=[pltpu.VMEM((2, rows, W_R), F32),
                        pltpu.VMEM((2, rows, W_R), F32),
                        pltpu.VMEM((2, W_R // V7X_LANES, rows, V7X_LANES), F32),
                        pltpu.VMEM((2, nb, W_R), F32)],
        compiler_params=pltpu.CompilerParams(
            dimension_semantics=("arbitrary",), vmem_limit_bytes=_vmem_limit(est)),
        name="lru",
    )(p, p, p, p, p, p, cw, cb, lam, wg, bg, h0)


def _mixer_kernel(x_ref, yf_ref, yb_ref, sh_ref, sc_ref, gt_ref, g1_ref, win_ref, gv_ref,
                  ws_ref, bs_ref, wpr_ref, wpg_ref, wout_ref, o_ref, *, tm):
    x = x_ref[...]
    h = (_rms(x, g1_ref[...]) * (1.0 + sc_ref[...]) + sh_ref[...]).astype(BF16)

    def proj(off, width):
        lo = off - W_R
        return jnp.dot(h, win_ref[:, lo:lo + width], preferred_element_type=F32)

    y_lru = yf_ref[...].astype(F32) + yb_ref[...].astype(F32)
    y_r = (_gelu(proj(OFF_RG, W_R)) * y_lru).astype(BF16)
    part_r = _sigmoid(proj(OFF_GR, D_MODEL)) * jnp.dot(y_r, wpr_ref[...], preferred_element_type=F32)

    v = _rms(_gelu(proj(OFF_V, W_G)), gv_ref[...]).astype(BF16)
    chunks = []
    for n in range(tm // CHUNK):
        heads = []
        for hh in range(H_G):
            blk = v[n * CHUNK:(n + 1) * CHUNK, hh * GC:(hh + 1) * GC]
            heads.append(jnp.dot(ws_ref[hh], blk, preferred_element_type=F32) + bs_ref[hh])
        chunks.append(jnp.concatenate(heads, axis=1))
    s = jnp.concatenate(chunks, axis=0)
    y_g = (_gelu(proj(OFF_U, W_G)) * s).astype(BF16)
    part_g = _sigmoid(proj(OFF_GG, D_MODEL)) * jnp.dot(y_g, wpg_ref[...], preferred_element_type=F32)

    merged = (part_r + part_g).astype(BF16)
    o_ref[...] = x + gt_ref[...] * jnp.dot(merged, wout_ref[...], preferred_element_type=F32)


def _mixer_call(x, yf, yb, shift, scale, gate, g1, w_rest, gv, ws, bs, wpr, wpg, wout):
    nb, t, d = x.shape
    tm = MIX_ROWS
    n_rest = N_IN - W_R
    tile = lambda width: pl.BlockSpec((None, tm, width), lambda b, i: (b, i, 0))
    vec = pl.BlockSpec((None, 1, d), lambda b, i: (b, 0, 0))
    est = (4 * tm * d * 4 + 4 * tm * W_R * 2
           + (d * n_rest + W_R * d + 2 * d * d + H_G * CHUNK * CHUNK) * 2 + H_G * CHUNK * GC * 4
           + 6 * tm * W_R * 4)
    return pl.pallas_call(
        functools.partial(_mixer_kernel, tm=tm),
        grid=(nb, t // tm),
        in_specs=[tile(d), tile(W_R), tile(W_R), vec, vec, vec,
                  _const_spec((1, d)), _const_spec((d, n_rest)), _const_spec((1, W_G)),
                  _const_spec((H_G, CHUNK, CHUNK)), _const_spec((H_G, CHUNK, GC)),
                  _const_spec((W_R, d)), _const_spec((W_G, d)), _const_spec((d, d))],
        out_specs=tile(d),
        out_shape=jax.ShapeDtypeStruct((nb, t, d), F32),
        compiler_params=pltpu.CompilerParams(
            dimension_semantics=("arbitrary", "arbitrary"), vmem_limit_bytes=_vmem_limit(est)),
        name="mixer",
    )(x, yf, yb, shift, scale, gate, g1, w_rest, gv, ws, bs, wpr, wpg, wout)


def _ffn_kernel(xm_ref, xp_ref, xn_ref, sh_ref, sc_ref, gt_ref, g2_ref, wg_ref, wv_ref, cw_ref,
                cb_ref, wd_ref, gf_ref, o_ref, h_s, acc_s, *, tm, n_tiles):
    i = pl.program_id(1)
    rows = tm + 2 * GRID_W
    xa = jnp.concatenate([xp_ref[...], xm_ref[...], xn_ref[...]], axis=0)
    h_s[...] = (_rms(xa, g2_ref[...]) * (1.0 + sc_ref[...]) + sh_ref[...]).astype(BF16)
    acc_s[...] = jnp.zeros_like(acc_s)

    ridx = lax.broadcasted_iota(jnp.int32, (rows, FFN_COLS), 0)
    col = ridx & (GRID_W - 1)
    first_valid = jnp.where(i > 0, 0, GRID_W)
    last_valid = jnp.where(i < n_tiles - 1, rows, tm + GRID_W)
    in_grid = (ridx >= first_valid) & (ridx < last_valid)
    at_left = col == 0
    at_right = col == GRID_W - 1

    def chunk(c, carry):
        g = jnp.dot(h_s[...], wg_ref[c], preferred_element_type=F32)
        g = jnp.where(in_grid, g, 0.0)
        g_l = jnp.where(at_left, 0.0, pltpu.roll(g, 1, 0))
        g_r = jnp.where(at_right, 0.0, pltpu.roll(g, rows - 1, 0))
        cw = cw_ref[c]
        conv = cb_ref[c]
        for dr in range(3):
            lo = GRID_W * dr
            conv = (conv + cw[3 * dr:3 * dr + 1, :] * g_l[lo:lo + tm]
                    + cw[3 * dr + 1:3 * dr + 2, :] * g[lo:lo + tm]
                    + cw[3 * dr + 2:3 * dr + 3, :] * g_r[lo:lo + tm])
        val = jnp.dot(h_s[GRID_W:GRID_W + tm, :], wv_ref[c], preferred_element_type=F32)
        act = (_gelu(conv) * val).astype(BF16)
        acc_s[...] += jnp.dot(act, wd_ref[c], preferred_element_type=F32)
        return carry

    lax.fori_loop(0, D_FF // FFN_COLS, chunk, 0)
    o_ref[...] = _rms(xm_ref[...] + gt_ref[...] * acc_s[...], gf_ref[...])


def _ffn_call(x1, shift, scale, gate, g2, wg3, wv3, cw3, cb3, wd3, gf):
    nb, t, d = x1.shape
    tm = FFN_ROWS
    n_tiles = t // tm
    r_h = tm // GRID_W
    n_h = t // GRID_W
    nc = D_FF // FFN_COLS
    vec = pl.BlockSpec((None, 1, d), lambda b, i: (b, 0, 0))
    est = (2 * 2 * tm * d * 4 + 4 * GRID_W * d * 4 + (2 * d * D_FF + D_FF * d) * 2
           + (tm + 2 * GRID_W) * d * 2 + tm * d * 4 + 8 * (tm + 2 * GRID_W) * FFN_COLS * 4)
    return pl.pallas_call(
        functools.partial(_ffn_kernel, tm=tm, n_tiles=n_tiles),
        grid=(nb, n_tiles),
        in_specs=[pl.BlockSpec((None, tm, d), lambda b, i: (b, i, 0)),
                  pl.BlockSpec((None, GRID_W, d), lambda b, i: (b, jnp.maximum(i * r_h - 1, 0), 0)),
                  pl.BlockSpec((None, GRID_W, d),
                               lambda b, i: (b, jnp.minimum((i + 1) * r_h, n_h - 1), 0)),
                  vec, vec, vec, _const_spec((1, d)),
                  _const_spec((nc, d, FFN_COLS)), _const_spec((nc, d, FFN_COLS)),
                  _const_spec((nc, 9, FFN_COLS)), _const_spec((nc, 1, FFN_COLS)),
                  _const_spec((nc, FFN_COLS, d)), _const_spec((1, d))],
        out_specs=pl.BlockSpec((None, tm, d), lambda b, i: (b, i, 0)),
        out_shape=jax.ShapeDtypeStruct((nb, t, d), F32),
        scratch_shapes=[pltpu.VMEM((tm + 2 * GRID_W, d), BF16), pltpu.VMEM((tm, d), F32)],
        compiler_params=pltpu.CompilerParams(
            dimension_semantics=("arbitrary", "arbitrary"), vmem_limit_bytes=_vmem_limit(est)),
        name="ffn",
    )(x1, x1, x1, shift, scale, gate, g2, wg3, wv3, cw3, cb3, wd3, gf)


def kernel(x, c, ctx, c_ctx, w_mod, b_mod, g_norm1, w_in, conv_w, conv_b, lru_lam, lru_wa, lru_ba,
           lru_wx, lru_bx, g_v, w_s, b_s, w_pr, w_pg, w_out, g_norm2, w_up, ffn_conv_w, ffn_conv_b,
           w_down, g_final):
    nb, t, d = x.shape
    depth = w_in.shape[0]
    assert depth == 1 and nb == V7X_SUBLANES and d == D_MODEL
    assert t % max(LRU_STEPS, MIX_ROWS, FFN_ROWS) == 0 and ctx.shape[1] % LRU_STEPS == 0
    l = 0

    pad = jnp.zeros((2 * V7X_SUBLANES - nb - 1, d), F32)
    cc = jnp.concatenate([c, c_ctx[None, :], pad], axis=0)
    m = _mod_call(cc, w_mod[l], b_mod[l][None, :]).reshape(cc.shape[0], N_MOD, d)
    m_x = m[:nb]
    m_c = jnp.broadcast_to(m[nb][None], (nb, N_MOD, d))

    w_in_b = w_in[l].astype(BF16)
    w_r, w_rest = w_in_b[:, :W_R], w_in_b[:, W_R:]
    g1 = g_norm1[l][None, :]
    wg = jnp.concatenate([lru_wa[l], lru_wx[l]], axis=-1).astype(BF16)
    bg = jnp.stack([lru_ba[l][0], lru_bx[l][0], lru_ba[l][1], lru_bx[l][1]], axis=0)
    lru_args = (conv_w[l], conv_b[l][None, :], lru_lam[l], wg, bg)

    pc = _inproj_call(ctx, m_c[:, 0], m_c[:, 1], g1, w_r)
    _, _, h_ctx = _lru_call(pc, nb, *lru_args, jnp.zeros((2, nb, W_R), F32))

    px = _inproj_call(x, m_x[:, 0], m_x[:, 1], g1, w_r)
    yf, yb, _ = _lru_call(px, nb, *lru_args, h_ctx)

    bs_e = jnp.broadcast_to(b_s[l].T[:, :, None], (H_G, CHUNK, GC))
    x1 = _mixer_call(x, yf, yb, m_x[:, 0:1], m_x[:, 1:2], m_x[:, 2:3], g1, w_rest,
                     g_v[l][None, :], w_s[l].astype(BF16), bs_e,
                     w_pr[l].astype(BF16), w_pg[l].astype(BF16), w_out[l].astype(BF16))

    nc = D_FF // FFN_COLS
    w_up_b = w_up[l].astype(BF16)
    wg3 = w_up_b[:, :D_FF].reshape(d, nc, FFN_COLS).transpose(1, 0, 2)
    wv3 = w_up_b[:, D_FF:].reshape(d, nc, FFN_COLS).transpose(1, 0, 2)
    cw3 = ffn_conv_w[l].reshape(9, nc, FFN_COLS).transpose(1, 0, 2)
    cb3 = ffn_conv_b[l].reshape(nc, 1, FFN_COLS)
    wd3 = w_down[l].astype(BF16).reshape(nc, FFN_COLS, d)
    return _ffn_call(x1, m_x[:, 3:4], m_x[:, 4:5], m_x[:, 5:6], g_norm2[l][None, :],
                     wg3, wv3, cw3, cb3, wd3, g_final[None, :])
```

```python
import functools
import math

import jax
import jax.numpy as jnp
from jax import lax
from jax.experimental import pallas as pl
from jax.experimental.pallas import tpu as pltpu

D_MODEL = 1024
GRID_W = 64
W_R = 1280
H_R = 5
BW = W_R // H_R
LRU_C = 8.0
CONV_R = 4
W_G = 1024
H_G = 8
GC = W_G // H_G
CHUNK = 128
D_FF = 2816
N_MOD = 6
EPS = 1e-6
RSQRT_FLOOR = 1e-30
OFF_RG = W_R
OFF_U = OFF_RG + W_R
OFF_V = OFF_U + W_G
OFF_GR = OFF_V + W_G
OFF_GG = OFF_GR + D_MODEL
N_IN = OFF_GG + D_MODEL

V7X_LANES = 128
V7X_SUBLANES = 8
V7X_MXU_WIDTH = 256
V7X_VMEM_BYTES = 64 * 1024 * 1024

F32 = jnp.float32
BF16 = jnp.bfloat16

LRU_STEPS = 64
MIX_ROWS = 256
FFN_ROWS = 512
FFN_COLS = V7X_MXU_WIDTH
MOD_COLS = 1024


def _vmem_limit(nbytes):
    return int(min(max(2 * nbytes, 16 * 1024 * 1024), V7X_VMEM_BYTES - 8 * 1024 * 1024))


def _gelu(x):
    k = math.sqrt(2.0 / math.pi)
    hx = 0.5 * x
    return hx + hx * jnp.tanh(x * (k + (k * 0.044715) * (x * x)))


def _sigmoid(x):
    return 1.0 / (1.0 + jnp.exp2(x * (-math.log2(math.e))))


def _rms(x, g):
    return x * lax.rsqrt(jnp.mean(x * x, axis=-1, keepdims=True) + EPS) * g


def _const_spec(shape):
    nd = len(shape)
    return pl.BlockSpec(shape, lambda *_: (0,) * nd, pipeline_mode=pl.Buffered(1))


def _mod_kernel(c_ref, w_ref, b_ref, o_ref):
    c = c_ref[...]
    s = c * _sigmoid(c)
    o_ref[...] = jnp.dot(s, w_ref[...], preferred_element_type=F32,
                         precision=lax.Precision.HIGHEST) + b_ref[...]


def _mod_call(cc, w, b):
    rows, d = cc.shape
    n = w.shape[1]
    return pl.pallas_call(
        _mod_kernel,
        grid=(n // MOD_COLS,),
        in_specs=[pl.BlockSpec((rows, d), lambda j: (0, 0)),
                  pl.BlockSpec((d, MOD_COLS), lambda j: (0, j)),
                  pl.BlockSpec((1, MOD_COLS), lambda j: (0, j))],
        out_specs=pl.BlockSpec((rows, MOD_COLS), lambda j: (0, j)),
        out_shape=jax.ShapeDtypeStruct((rows, n), F32),
        compiler_params=pltpu.CompilerParams(
            dimension_semantics=("arbitrary",),
            vmem_limit_bytes=_vmem_limit(2 * d * MOD_COLS * 4)),
        name="mod",
    )(cc, w, b)


def _inproj_kernel(x_ref, sh_ref, sc_ref, g_ref, w_ref, o_ref, hs_ref, *, steps):
    nb = x_ref.shape[0]
    g = g_ref[...]
    for b in range(nb):
        hb = _rms(x_ref[b], g) * (1.0 + sc_ref[b:b + 1, :]) + sh_ref[b:b + 1, :]
        for s in range(D_MODEL // V7X_LANES):
            hs_ref[s, pl.ds(b, steps, stride=nb), :] = hb[:, s * V7X_LANES:(s + 1) * V7X_LANES]
    h = jnp.concatenate([hs_ref[s] for s in range(D_MODEL // V7X_LANES)], axis=1).astype(BF16)
    o_ref[...] = jnp.dot(h, w_ref[...], preferred_element_type=F32)


def _inproj_call(x, shift, scale, g, w_r):
    nb, t, d = x.shape
    steps = LRU_STEPS
    rows = steps * nb
    est = 2 * nb * steps * d * 4 + d * W_R * 2 + 2 * rows * W_R * 4 + rows * d * 4
    return pl.pallas_call(
        functools.partial(_inproj_kernel, steps=steps),
        grid=(t // steps,),
        in_specs=[pl.BlockSpec((nb, steps, d), lambda i: (0, i, 0)),
                  _const_spec((nb, d)), _const_spec((nb, d)), _const_spec((1, d)),
                  _const_spec((d, W_R))],
        out_specs=pl.BlockSpec((rows, W_R), lambda i: (i, 0)),
        out_shape=jax.ShapeDtypeStruct((t * nb, W_R), F32),
        scratch_shapes=[pltpu.VMEM((d // V7X_LANES, rows, V7X_LANES), F32)],
        compiler_params=pltpu.CompilerParams(
            dimension_semantics=("arbitrary",), vmem_limit_bytes=_vmem_limit(est)),
        name="inproj",
    )(x, shift, scale, g, w_r)


def _lru_kernel(pfm, pfp, pfn, pbm, pbp, pbn, cw_ref, cb_ref, lam_ref, wg_ref, bg_ref, h0_ref,
                yf_ref, yb_ref, hl_ref, a_s, b_s, y_s, hc, *, steps, n_tiles):
    i = pl.program_id(0)
    nb = V7X_SUBLANES
    rows = steps * nb
    n_slab = W_R // V7X_LANES

    @pl.when(i == 0)
    def _():
        hc[...] = h0_ref[...]

    z = -lam_ref[...]
    softplus = jnp.maximum(z, 0.0) + jnp.log1p(jnp.exp(-jnp.abs(z)))
    decay = (-LRU_C * math.log2(math.e)) * softplus
    cw = cw_ref[...]
    cb = cb_ref[...]

    def coefficients(d, tile, pm, pp, pn):
        keep_prev = jnp.where(tile > 0, 1.0, 0.0)
        keep_next = jnp.where(tile < n_tiles - 1, 1.0, 0.0)
        prev2 = pp[...] * keep_prev
        next1 = pn[0:nb, :] * keep_next
        pe = jnp.concatenate([prev2, pm[...], next1], axis=0)
        xr = cb
        for k in range(CONV_R):
            xr = xr + cw[k:k + 1, :] * pe[k * nb:k * nb + rows]
        xrb = xr.astype(BF16)
        for h in range(H_R):
            cs = slice(h * BW, (h + 1) * BW)
            gates = jnp.dot(xrb[:, cs], wg_ref[d, h], preferred_element_type=F32)
            r = _sigmoid(gates[:, :BW] + bg_ref[2 * d:2 * d + 1, cs])
            ig = _sigmoid(gates[:, BW:] + bg_ref[2 * d + 1:2 * d + 2, cs])
            a = jnp.exp2(r * decay[d:d + 1, cs])
            a_s[d, :, cs] = a
            y = 1.0 - a * a
            b_s[d, :, cs] = (y * lax.rsqrt(jnp.maximum(y, RSQRT_FLOOR))) * (ig * xr[:, cs])

    coefficients(0, i, pfm, pfp, pfn)
    coefficients(1, n_tiles - 1 - i, pbm, pbp, pbn)

    def step(t, carry):
        hf, hb = carry
        rf = pl.ds(pl.multiple_of(t * nb, nb), nb)
        rb = pl.ds(pl.multiple_of((steps - 1 - t) * nb, nb), nb)
        hf = a_s[0, rf, :] * hf + b_s[0, rf, :]
        hb = a_s[1, rb, :] * hb + b_s[1, rb, :]
        for s in range(n_slab):
            y_s[0, s, rf, :] = hf[:, s * V7X_LANES:(s + 1) * V7X_LANES]
            y_s[1, s, rb, :] = hb[:, s * V7X_LANES:(s + 1) * V7X_LANES]
        return hf, hb

    hf, hb = lax.fori_loop(0, steps, step, (hc[0], hc[1]), unroll=4)
    hc[0] = hf
    hc[1] = hb
    hl_ref[0] = hf
    hl_ref[1] = hb

    for b in range(nb):
        yf_ref[b] = jnp.concatenate(
            [y_s[0, s, pl.ds(b, steps, stride=nb), :] for s in range(n_slab)], axis=1).astype(BF16)
        yb_ref[b] = jnp.concatenate(
            [y_s[1, s, pl.ds(b, steps, stride=nb), :] for s in range(n_slab)], axis=1).astype(BF16)


def _lru_call(p, nb, cw, cb, lam, wg, bg, h0):
    steps = LRU_STEPS
    rows = steps * nb
    t = p.shape[0] // nb
    n_tiles = t // steps
    halo = (CONV_R - 2) * nb
    r_h = rows // halo
    n_h = p.shape[0] // halo

    def main(tile):
        return pl.BlockSpec((rows, W_R), lambda i: (tile(i), 0))

    def prev(tile):
        return pl.BlockSpec((halo, W_R), lambda i: (jnp.maximum(tile(i) * r_h - 1, 0), 0))

    def nxt(tile):
        return pl.BlockSpec((halo, W_R), lambda i: (jnp.minimum((tile(i) + 1) * r_h, n_h - 1), 0))

    fwd = lambda i: i
    bwd = lambda i: n_tiles - 1 - i
    est = (2 * 2 * rows * W_R * 4 + 2 * 2 * nb * steps * W_R * 2 + 3 * 2 * rows * W_R * 4
           + 2 * H_R * BW * 2 * BW * 2 + 4 * rows * W_R * 4)
    return pl.pallas_call(
        functools.partial(_lru_kernel, steps=steps, n_tiles=n_tiles),
        grid=(n_tiles,),
        in_specs=[main(fwd), prev(fwd), nxt(fwd), main(bwd), prev(bwd), nxt(bwd),
                  _const_spec((CONV_R, W_R)), _const_spec((1, W_R)), _const_spec((2, W_R)),
                  _const_spec((2, H_R, BW, 2 * BW)), _const_spec((4, W_R)),
                  _const_spec((2, nb, W_R))],
        out_specs=[pl.BlockSpec((nb, steps, W_R), lambda i: (0, i, 0)),
                   pl.BlockSpec((nb, steps, W_R), lambda i: (0, n_tiles - 1 - i, 0)),
                   pl.BlockSpec((2, nb, W_R), lambda i: (0, 0, 0))],
        out_shape=[jax.ShapeDtypeStruct((nb, t, W_R), BF16),
                   jax.ShapeDtypeStruct((nb, t, W_R), BF16),
                   jax.ShapeDtypeStruct((2, nb, W_R), F32)],
        scratch_shapes=[pltpu.VMEM((2, rows, W_R), F32),
                        pltpu.VMEM((2, rows, W_R), F32),
                        pltpu.VMEM((2, W_R // V7X_LANES, rows, V7X_LANES), F32),
                        pltpu.VMEM((2, nb, W_R), F32)],
        compiler_params=pltpu.CompilerParams(
            dimension_semantics=("arbitrary",), vmem_limit_bytes=_vmem_limit(est)),
        name="lru",
    )(p, p, p, p, p, p, cw, cb, lam, wg, bg, h0)


def _mixer_kernel(x_ref, yf_ref, yb_ref, sh_ref, sc_ref, gt_ref, g1_ref, win_ref, gv_ref,
                  ws_ref, bs_ref, wpr_ref, wpg_ref, wout_ref, o_ref, *, tm):
    x = x_ref[...]
    h = (_rms(x, g1_ref[...]) * (1.0 + sc_ref[...]) + sh_ref[...]).astype(BF16)

    def proj(off, width):
        lo = off - W_R
        return jnp.dot(h, win_ref[:, lo:lo + width], preferred_element_type=F32)

    y_lru = yf_ref[...].astype(F32) + yb_ref[...].astype(F32)
    y_r = (_gelu(proj(OFF_RG, W_R)) * y_lru).astype(BF16)
    part_r = _sigmoid(proj(OFF_GR, D_MODEL)) * jnp.dot(y_r, wpr_ref[...], preferred_element_type=F32)

    v = _rms(_gelu(proj(OFF_V, W_G)), gv_ref[...]).astype(BF16)
    chunks = []
    for n in range(tm // CHUNK):
        heads = []
        for hh in range(H_G):
            blk = v[n * CHUNK:(n + 1) * CHUNK, hh * GC:(hh + 1) * GC]
            heads.append(jnp.dot(ws_ref[hh], blk, preferred_element_type=F32) + bs_ref[hh])
        chunks.append(jnp.concatenate(heads, axis=1))
    s = jnp.concatenate(chunks, axis=0)
    y_g = (_gelu(proj(OFF_U, W_G)) * s).astype(BF16)
    part_g = _sigmoid(proj(OFF_GG, D_MODEL)) * jnp.dot(y_g, wpg_ref[...], preferred_element_type=F32)

    merged = (part_r + part_g).astype(BF16)
    o_ref[...] = x + gt_ref[...] * jnp.dot(merged, wout_ref[...], preferred_element_type=F32)


def _mixer_call(x, yf, yb, shift, scale, gate, g1, w_rest, gv, ws, bs, wpr, wpg, wout):
    nb, t, d = x.shape
    tm = MIX_ROWS
    n_rest = N_IN - W_R
    tile = lambda width: pl.BlockSpec((None, tm, width), lambda b, i: (b, i, 0))
    vec = pl.BlockSpec((None, 1, d), lambda b, i: (b, 0, 0))
    est = (4 * tm * d * 4 + 4 * tm * W_R * 2
           + (d * n_rest + W_R * d + 2 * d * d + H_G * CHUNK * CHUNK) * 2 + H_G * CHUNK * GC * 4
           + 6 * tm * W_R * 4)
    return pl.pallas_call(
        functools.partial(_mixer_kernel, tm=tm),
        grid=(nb, t // tm),
        in_specs=[tile(d), tile(W_R), tile(W_R), vec, vec, vec,
                  _const_spec((1, d)), _const_spec((d, n_rest)), _const_spec((1, W_G)),
                  _const_spec((H_G, CHUNK, CHUNK)), _const_spec((H_G, CHUNK, GC)),
                  _const_spec((W_R, d)), _const_spec((W_G, d)), _const_spec((d, d))],
        out_specs=tile(d),
        out_shape=jax.ShapeDtypeStruct((nb, t, d), F32),
        compiler_params=pltpu.CompilerParams(
            dimension_semantics=("arbitrary", "arbitrary"), vmem_limit_bytes=_vmem_limit(est)),
        name="mixer",
    )(x, yf, yb, shift, scale, gate, g1, w_rest, gv, ws, bs, wpr, wpg, wout)


def _ffn_kernel(xm_ref, xp_ref, xn_ref, sh_ref, sc_ref, gt_ref, g2_ref, wg_ref, wv_ref, cw_ref,
                cb_ref, wd_ref, gf_ref, o_ref, h_s, act_s, *, tm, n_tiles):
    i = pl.program_id(1)
    rows = tm + 2 * GRID_W
    g2 = g2_ref[...]
    sc = 1.0 + sc_ref[...]
    sh = sh_ref[...]

    def modulated(x):
        return _rms(x, g2) * sc + sh

    keep_prev = jnp.where(i > 0, 1.0, 0.0)
    keep_next = jnp.where(i < n_tiles - 1, 1.0, 0.0)
    h_s[0:GRID_W, :] = (modulated(xp_ref[...]) * keep_prev).astype(BF16)
    h_s[GRID_W:GRID_W + tm, :] = modulated(xm_ref[...]).astype(BF16)
    h_s[GRID_W + tm:rows, :] = (modulated(xn_ref[...]) * keep_next).astype(BF16)

    col = lax.broadcasted_iota(jnp.int32, (rows, FFN_COLS), 0) & (GRID_W - 1)
    has_left = jnp.where(col == 0, 0.0, 1.0).astype(BF16)
    has_right = jnp.where(col == GRID_W - 1, 0.0, 1.0).astype(BF16)

    for c in range(D_FF // FFN_COLS):
        g = jnp.dot(h_s[...], wg_ref[c], preferred_element_type=F32)
        g_c = g.astype(BF16)
        g_l = pltpu.roll(g, 1, 0).astype(BF16) * has_left
        g_r = pltpu.roll(g, rows - 1, 0).astype(BF16) * has_right
        cw = cw_ref[c]
        conv = cb_ref[c]
        for dr in range(3):
            lo = GRID_W * dr
            conv = (conv + cw[3 * dr:3 * dr + 1, :] * g_l[lo:lo + tm]
                    + cw[3 * dr + 1:3 * dr + 2, :] * g_c[lo:lo + tm]
                    + cw[3 * dr + 2:3 * dr + 3, :] * g_r[lo:lo + tm])
        val = jnp.dot(h_s[GRID_W:GRID_W + tm, :], wv_ref[c], preferred_element_type=F32)
        act_s[:, c * FFN_COLS:(c + 1) * FFN_COLS] = (_gelu(conv.astype(F32)) * val).astype(BF16)
    down = jnp.dot(act_s[...], wd_ref[...], preferred_element_type=F32)
    o_ref[...] = _rms(xm_ref[...] + gt_ref[...] * down, gf_ref[...])


def _ffn_call(x1, shift, scale, gate, g2, wg3, wv3, cw3, cb3, wd3, gf):
    nb, t, d = x1.shape
    tm = FFN_ROWS
    n_tiles = t // tm
    r_h = tm // GRID_W
    n_h = t // GRID_W
    nc = D_FF // FFN_COLS
    vec = pl.BlockSpec((None, 1, d), lambda b, i: (b, 0, 0))
    est = (2 * 2 * tm * d * 4 + 4 * GRID_W * d * 4 + (2 * d * D_FF + D_FF * d) * 2
           + (tm + 2 * GRID_W) * d * 2 + tm * d * 4 + 8 * (tm + 2 * GRID_W) * FFN_COLS * 4)
    return pl.pallas_call(
        functools.partial(_ffn_kernel, tm=tm, n_tiles=n_tiles),
        grid=(nb, n_tiles),
        in_specs=[pl.BlockSpec((None, tm, d), lambda b, i: (b, i, 0)),
                  pl.BlockSpec((None, GRID_W, d), lambda b, i: (b, jnp.maximum(i * r_h - 1, 0), 0)),
                  pl.BlockSpec((None, GRID_W, d),
                               lambda b, i: (b, jnp.minimum((i + 1) * r_h, n_h - 1), 0)),
                  vec, vec, vec, _const_spec((1, d)),
                  _const_spec((nc, d, FFN_COLS)), _const_spec((nc, d, FFN_COLS)),
                  _const_spec((nc, 9, FFN_COLS)), _const_spec((nc, 1, FFN_COLS)),
                  _const_spec((D_FF, d)), _const_spec((1, d))],
        out_specs=pl.BlockSpec((None, tm, d), lambda b, i: (b, i, 0)),
        out_shape=jax.ShapeDtypeStruct((nb, t, d), F32),
        scratch_shapes=[pltpu.VMEM((tm + 2 * GRID_W, d), BF16), pltpu.VMEM((tm, D_FF), BF16)],
        compiler_params=pltpu.CompilerParams(
            dimension_semantics=("arbitrary", "arbitrary"), vmem_limit_bytes=_vmem_limit(est)),
        name="ffn",
    )(x1, x1, x1, shift, scale, gate, g2, wg3, wv3, cw3, cb3, wd3, gf)


def kernel(x, c, ctx, c_ctx, w_mod, b_mod, g_norm1, w_in, conv_w, conv_b, lru_lam, lru_wa, lru_ba,
           lru_wx, lru_bx, g_v, w_s, b_s, w_pr, w_pg, w_out, g_norm2, w_up, ffn_conv_w, ffn_conv_b,
           w_down, g_final):
    nb, t, d = x.shape
    depth = w_in.shape[0]
    assert depth == 1 and nb == V7X_SUBLANES and d == D_MODEL
    assert t % max(LRU_STEPS, MIX_ROWS, FFN_ROWS) == 0 and ctx.shape[1] % LRU_STEPS == 0
    l = 0

    pad = jnp.zeros((2 * V7X_SUBLANES - nb - 1, d), F32)
    cc = jnp.concatenate([c, c_ctx[None, :], pad], axis=0)
    m = _mod_call(cc, w_mod[l], b_mod[l][None, :]).reshape(cc.shape[0], N_MOD, d)
    m_x = m[:nb]
    m_c = jnp.broadcast_to(m[nb][None], (nb, N_MOD, d))

    w_in_b = w_in[l].astype(BF16)
    w_r, w_rest = w_in_b[:, :W_R], w_in_b[:, W_R:]
    g1 = g_norm1[l][None, :]
    wg = jnp.concatenate([lru_wa[l], lru_wx[l]], axis=-1).astype(BF16)
    bg = jnp.stack([lru_ba[l][0], lru_bx[l][0], lru_ba[l][1], lru_bx[l][1]], axis=0)
    lru_args = (conv_w[l], conv_b[l][None, :], lru_lam[l], wg, bg)

    pc = _inproj_call(ctx, m_c[:, 0], m_c[:, 1], g1, w_r)
    _, _, h_ctx = _lru_call(pc, nb, *lru_args, jnp.zeros((2, nb, W_R), F32))

    px = _inproj_call(x, m_x[:, 0], m_x[:, 1], g1, w_r)
    yf, yb, _ = _lru_call(px, nb, *lru_args, h_ctx)

    bs_e = jnp.broadcast_to(b_s[l].T[:, :, None], (H_G, CHUNK, GC))
    x1 = _mixer_call(x, yf, yb, m_x[:, 0:1], m_x[:, 1:2], m_x[:, 2:3], g1, w_rest,
                     g_v[l][None, :], w_s[l].astype(BF16), bs_e,
                     w_pr[l].astype(BF16), w_pg[l].astype(BF16), w_out[l].astype(BF16))

    nc = D_FF // FFN_COLS
    w_up_b = w_up[l].astype(BF16)
    wg3 = w_up_b[:, :D_FF].reshape(d, nc, FFN_COLS).transpose(1, 0, 2)
    wv3 = w_up_b[:, D_FF:].reshape(d, nc, FFN_COLS).transpose(1, 0, 2)
    cw3 = ffn_conv_w[l].reshape(9, nc, FFN_COLS).transpose(1, 0, 2).astype(BF16)
    cb3 = ffn_conv_b[l].reshape(nc, 1, FFN_COLS).astype(BF16)
    wd3 = w_down[l].astype(BF16)
    return _ffn_call(x1, m_x[:, 3:4], m_x[:, 4:5], m_x[:, 5:6], g_norm2[l][None, :],
                     wg3, wv3, cw3, cb3, wd3, g_final[None, :])
```

```python
import functools
import math

import jax
import jax.numpy as jnp
from jax import lax
from jax.experimental import pallas as pl
from jax.experimental.pallas import tpu as pltpu

D_MODEL = 1024
GRID_W = 64
W_R = 1280
H_R = 5
BW = W_R // H_R
LRU_C = 8.0
CONV_R = 4
W_G = 1024
H_G = 8
GC = W_G // H_G
CHUNK = 128
D_FF = 2816
N_MOD = 6
EPS = 1e-6
RSQRT_FLOOR = 1e-30
OFF_RG = W_R
OFF_U = OFF_RG + W_R
OFF_V = OFF_U + W_G
OFF_GR = OFF_V + W_G
OFF_GG = OFF_GR + D_MODEL
N_IN = OFF_GG + D_MODEL

V7X_LANES = 128
V7X_SUBLANES = 8
V7X_MXU_WIDTH = 256
V7X_VMEM_BYTES = 64 * 1024 * 1024

F32 = jnp.float32
BF16 = jnp.bfloat16

INPROJ_STEPS = 64
LRU_STEPS = 32
MIX_ROWS = 512
FFN_ROWS = 512
FFN_COLS = V7X_MXU_WIDTH
MOD_COLS = 1024


def _vmem_limit(nbytes):
    return int(min(max(2 * nbytes, 16 * 1024 * 1024), V7X_VMEM_BYTES - 8 * 1024 * 1024))


def _gelu(x):
    k = math.sqrt(2.0 / math.pi)
    hx = 0.5 * x
    return hx + hx * jnp.tanh(x * (k + (k * 0.044715) * (x * x)))


def _sigmoid(x):
    return 1.0 / (1.0 + jnp.exp2(x * (-math.log2(math.e))))


def _rms(x, g):
    return x * lax.rsqrt(jnp.mean(x * x, axis=-1, keepdims=True) + EPS) * g


def _const_spec(shape):
    nd = len(shape)
    return pl.BlockSpec(shape, lambda *_: (0,) * nd, pipeline_mode=pl.Buffered(1))


def _mod_kernel(c_ref, w_ref, b_ref, o_ref):
    c = c_ref[...]
    s = c * _sigmoid(c)
    o_ref[...] = jnp.dot(s, w_ref[...], preferred_element_type=F32,
                         precision=lax.Precision.HIGHEST) + b_ref[...]


def _mod_call(cc, w, b):
    rows, d = cc.shape
    n = w.shape[1]
    return pl.pallas_call(
        _mod_kernel,
        grid=(n // MOD_COLS,),
        in_specs=[pl.BlockSpec((rows, d), lambda j: (0, 0)),
                  pl.BlockSpec((d, MOD_COLS), lambda j: (0, j)),
                  pl.BlockSpec((1, MOD_COLS), lambda j: (0, j))],
        out_specs=pl.BlockSpec((rows, MOD_COLS), lambda j: (0, j)),
        out_shape=jax.ShapeDtypeStruct((rows, n), F32),
        compiler_params=pltpu.CompilerParams(
            dimension_semantics=("arbitrary",),
            vmem_limit_bytes=_vmem_limit(2 * d * MOD_COLS * 4)),
        name="mod",
    )(cc, w, b)


def _inproj_kernel(x_ref, sh_ref, sc_ref, g_ref, w_ref, o_ref, hs_ref, *, steps):
    nb = x_ref.shape[0]
    g = g_ref[...]
    for b in range(nb):
        hb = _rms(x_ref[b], g) * (1.0 + sc_ref[b:b + 1, :]) + sh_ref[b:b + 1, :]
        for s in range(D_MODEL // V7X_LANES):
            hs_ref[s, pl.ds(b, steps, stride=nb), :] = hb[:, s * V7X_LANES:(s + 1) * V7X_LANES]
    h = jnp.concatenate([hs_ref[s] for s in range(D_MODEL // V7X_LANES)], axis=1).astype(BF16)
    o_ref[...] = jnp.dot(h, w_ref[...], preferred_element_type=F32)


def _inproj_call(x, shift, scale, g, w_r):
    nb, t, d = x.shape
    steps = INPROJ_STEPS
    rows = steps * nb
    est = 2 * nb * steps * d * 4 + d * W_R * 2 + 2 * rows * W_R * 4 + rows * d * 4
    return pl.pallas_call(
        functools.partial(_inproj_kernel, steps=steps),
        grid=(t // steps,),
        in_specs=[pl.BlockSpec((nb, steps, d), lambda i: (0, i, 0)),
                  _const_spec((nb, d)), _const_spec((nb, d)), _const_spec((1, d)),
                  _const_spec((d, W_R))],
        out_specs=pl.BlockSpec((rows, W_R), lambda i: (i, 0)),
        out_shape=jax.ShapeDtypeStruct((t * nb, W_R), F32),
        scratch_shapes=[pltpu.VMEM((d // V7X_LANES, rows, V7X_LANES), F32)],
        compiler_params=pltpu.CompilerParams(
            dimension_semantics=("arbitrary",), vmem_limit_bytes=_vmem_limit(est)),
        name="inproj",
    )(x, shift, scale, g, w_r)


def _lru_pieces(i, p_refs, cw_ref, cb_ref, lam_ref, wg_ref, bg_ref, a_s, b_s, *, steps, n_tiles):
    nb = V7X_SUBLANES
    rows = steps * nb
    z = -lam_ref[...]
    softplus = jnp.maximum(z, 0.0) + jnp.log1p(jnp.exp(-jnp.abs(z)))
    decay = (-LRU_C * math.log2(math.e)) * softplus

    def piece(d, tile, pm, pp, pn, h):
        cs = slice(h * BW, (h + 1) * BW)
        keep_prev = jnp.where(tile > 0, 1.0, 0.0)
        keep_next = jnp.where(tile < n_tiles - 1, 1.0, 0.0)
        prev2 = pp[:, cs] * keep_prev
        next1 = pn[0:nb, cs] * keep_next
        pe = jnp.concatenate([prev2, pm[:, cs], next1], axis=0)
        xr = cb_ref[:, cs]
        for k in range(CONV_R):
            xr = xr + cw_ref[k:k + 1, cs] * pe[k * nb:k * nb + rows]
        gates = jnp.dot(xr.astype(BF16), wg_ref[d, h], preferred_element_type=F32)
        r = _sigmoid(gates[:, :BW] + bg_ref[2 * d:2 * d + 1, cs])
        ig = _sigmoid(gates[:, BW:] + bg_ref[2 * d + 1:2 * d + 2, cs])
        a = jnp.exp2(r * decay[d:d + 1, cs])
        a_s[d, :, cs] = a
        y = 1.0 - a * a
        b_s[d, :, cs] = (y * lax.rsqrt(jnp.maximum(y, RSQRT_FLOOR))) * (ig * xr)

    tiles = (i, n_tiles - 1 - i)
    return [functools.partial(piece, d, tiles[d], *p_refs[3 * d:3 * d + 3], h)
            for h in range(H_R) for d in range(2)]


def _lru_scan(a_s, b_s, y_s, hc, yf_ref, yb_ref, hl_ref, *, steps):
    nb = V7X_SUBLANES
    n_slab = W_R // V7X_LANES

    def step(t, carry):
        hf, hb = carry
        rf = pl.ds(pl.multiple_of(t * nb, nb), nb)
        rb = pl.ds(pl.multiple_of((steps - 1 - t) * nb, nb), nb)
        hf = a_s[0, rf, :] * hf + b_s[0, rf, :]
        hb = a_s[1, rb, :] * hb + b_s[1, rb, :]
        for s in range(n_slab):
            y_s[0, s, rf, :] = hf[:, s * V7X_LANES:(s + 1) * V7X_LANES]
            y_s[1, s, rb, :] = hb[:, s * V7X_LANES:(s + 1) * V7X_LANES]
        return hf, hb

    hf, hb = lax.fori_loop(0, steps, step, (hc[0], hc[1]), unroll=4)
    hc[0] = hf
    hc[1] = hb
    hl_ref[0] = hf
    hl_ref[1] = hb

    for b in range(nb):
        yf_ref[b] = jnp.concatenate(
            [y_s[0, s, pl.ds(b, steps, stride=nb), :] for s in range(n_slab)], axis=1).astype(BF16)
        yb_ref[b] = jnp.concatenate(
            [y_s[1, s, pl.ds(b, steps, stride=nb), :] for s in range(n_slab)], axis=1).astype(BF16)


def _gmlp_pieces(x_ref, sh_ref, sc_ref, g1_ref, win_ref, gv_ref, ws_ref, bs_ref, wpg_ref,
                 qrg_ref, gr_ref, pg_ref, *, tm):
    w = V7X_MXU_WIDTH
    st = {}

    def cols(j):
        return slice(j * w, (j + 1) * w)

    def proj(off, j):
        lo = off - W_R + j * w
        return jnp.dot(st["h"], win_ref[:, lo:lo + w], preferred_element_type=F32)

    def prep():
        st["h"] = (_rms(x_ref[...], g1_ref[...]) * (1.0 + sc_ref[...]) + sh_ref[...]).astype(BF16)
        st["v"] = [None] * (W_G // w)
        st["yg"] = [None] * (W_G // w)
        st["s"] = [None] * (tm // CHUNK)

    def v_cols(j):
        st["v"][j] = _gelu(proj(OFF_V, j))

    def v_norm():
        st["vn"] = _rms(jnp.concatenate(st["v"], axis=1), gv_ref[...]).astype(BF16)

    def mix(n):
        heads = []
        for hh in range(H_G):
            blk = st["vn"][n * CHUNK:(n + 1) * CHUNK, hh * GC:(hh + 1) * GC]
            heads.append(jnp.dot(ws_ref[hh], blk, preferred_element_type=F32) + bs_ref[hh])
        st["s"][n] = jnp.concatenate(heads, axis=1)

    def u_cols(j):
        s_j = jnp.concatenate([s_n[:, cols(j)] for s_n in st["s"]], axis=0)
        st["yg"][j] = (_gelu(proj(OFF_U, j)) * s_j).astype(BF16)

    def g_cols(j):
        if j == 0:
            st["ygc"] = jnp.concatenate(st["yg"], axis=1)
        acc = jnp.dot(st["ygc"], wpg_ref[:, cols(j)], preferred_element_type=F32)
        pg_ref[:, cols(j)] = (_sigmoid(proj(OFF_GG, j)) * acc).astype(BF16)

    def rg_cols(j):
        qrg_ref[:, cols(j)] = _gelu(proj(OFF_RG, j)).astype(BF16)

    def gr_cols(j):
        gr_ref[:, cols(j)] = _sigmoid(proj(OFF_GR, j)).astype(BF16)

    P = functools.partial
    return ([prep] + [P(v_cols, j) for j in range(W_G // w)] + [v_norm]
            + [P(rg_cols, j) for j in range(W_R // w)]
            + [P(mix, n) for n in range(tm // CHUNK)]
            + [P(u_cols, j) for j in range(W_G // w)]
            + [P(gr_cols, j) for j in range(D_MODEL // w)]
            + [P(g_cols, j) for j in range(D_MODEL // w)])


def _interleave(major, minor):
    out = []
    k = 0
    for idx, f in enumerate(major):
        out.append(f)
        while k < len(minor) and (k + 1) * len(major) <= (idx + 1) * len(minor):
            out.append(minor[k])
            k += 1
    return out + minor[k:]


N_LRU_IN = 12
N_GMLP_IN = 9
N_GMLP_OUT = 3


def _lru_kernel(*refs, steps, n_tiles, tm):
    n_extra_in = N_GMLP_IN if tm else 0
    n_extra_out = N_GMLP_OUT if tm else 0
    p_refs = refs[0:6]
    cw_ref, cb_ref, lam_ref, wg_ref, bg_ref, h0_ref = refs[6:N_LRU_IN]
    gmlp_in = refs[N_LRU_IN:N_LRU_IN + n_extra_in]
    outs = refs[N_LRU_IN + n_extra_in:]
    yf_ref, yb_ref, hl_ref = outs[0:3]
    gmlp_out = outs[3:3 + n_extra_out]
    a_s, b_s, y_s, hc = outs[3 + n_extra_out:]
    i = pl.program_id(0)

    @pl.when(i == 0)
    def _():
        hc[...] = h0_ref[...]

    lru = _lru_pieces(i, p_refs, cw_ref, cb_ref, lam_ref, wg_ref, bg_ref, a_s, b_s,
                      steps=steps, n_tiles=n_tiles)
    work = _interleave(_gmlp_pieces(*gmlp_in, *gmlp_out, tm=tm), lru) if tm else lru
    for piece in work:
        piece()
    _lru_scan(a_s, b_s, y_s, hc, yf_ref, yb_ref, hl_ref, steps=steps)


def _lru_call(p, nb, cw, cb, lam, wg, bg, h0, gmlp=None):
    steps = LRU_STEPS
    rows = steps * nb
    t = p.shape[0] // nb
    n_tiles = t // steps
    halo = (CONV_R - 2) * nb
    r_h = rows // halo
    n_h = p.shape[0] // halo

    def main(tile):
        return pl.BlockSpec((rows, W_R), lambda i: (tile(i), 0))

    def prev(tile):
        return pl.BlockSpec((halo, W_R), lambda i: (jnp.maximum(tile(i) * r_h - 1, 0), 0))

    def nxt(tile):
        return pl.BlockSpec((halo, W_R), lambda i: (jnp.minimum((tile(i) + 1) * r_h, n_h - 1), 0))

    fwd = lambda i: i
    bwd = lambda i: n_tiles - 1 - i
    in_specs = [main(fwd), prev(fwd), nxt(fwd), main(bwd), prev(bwd), nxt(bwd),
                _const_spec((CONV_R, W_R)), _const_spec((1, W_R)), _const_spec((2, W_R)),
                _const_spec((2, H_R, BW, 2 * BW)), _const_spec((4, W_R)),
                _const_spec((2, nb, W_R))]
    out_specs = [pl.BlockSpec((nb, steps, W_R), lambda i: (0, i, 0)),
                 pl.BlockSpec((nb, steps, W_R), lambda i: (0, n_tiles - 1 - i, 0)),
                 pl.BlockSpec((2, nb, W_R), lambda i: (0, 0, 0))]
    out_shape = [jax.ShapeDtypeStruct((nb, t, W_R), BF16),
                 jax.ShapeDtypeStruct((nb, t, W_R), BF16),
                 jax.ShapeDtypeStruct((2, nb, W_R), F32)]
    est = (2 * 2 * rows * W_R * 4 + 2 * 2 * nb * steps * W_R * 2 + 3 * 2 * rows * W_R * 4
           + 2 * H_R * BW * 2 * BW * 2 + 4 * rows * W_R * 4)
    args = (p, p, p, p, p, p, cw, cb, lam, wg, bg, h0)
    tm = 0
    if gmlp is not None:
        assert len(gmlp) == N_GMLP_IN
        d = gmlp[0].shape[2]
        n_rest = N_IN - W_R
        tm = (nb * t) // n_tiles
        per_b = t // tm
        assert tm % CHUNK == 0 and t % tm == 0
        tile = lambda width: pl.BlockSpec((None, tm, width), lambda i: (i // per_b, i % per_b, 0))
        vec = pl.BlockSpec((None, 1, d), lambda i: (i // per_b, 0, 0))
        in_specs += [tile(d), vec, vec, _const_spec((1, d)), _const_spec((d, n_rest)),
                     _const_spec((1, W_G)), _const_spec((H_G, CHUNK, CHUNK)),
                     _const_spec((H_G, CHUNK, GC)), _const_spec((W_G, d))]
        out_specs += [tile(W_R), tile(d), tile(d)]
        out_shape += [jax.ShapeDtypeStruct((nb, t, W_R), BF16),
                      jax.ShapeDtypeStruct((nb, t, d), BF16),
                      jax.ShapeDtypeStruct((nb, t, d), BF16)]
        est += (2 * tm * d * 4 + 2 * tm * (W_R + 2 * d) * 2 + (d * n_rest + W_G * d) * 2
                + 4 * tm * W_R * 4)
        args += tuple(gmlp)
    return pl.pallas_call(
        functools.partial(_lru_kernel, steps=steps, n_tiles=n_tiles, tm=tm),
        grid=(n_tiles,),
        in_specs=in_specs,
        out_specs=out_specs,
        out_shape=out_shape,
        scratch_shapes=[pltpu.VMEM((2, rows, W_R), F32),
                        pltpu.VMEM((2, rows, W_R), F32),
                        pltpu.VMEM((2, W_R // V7X_LANES, rows, V7X_LANES), F32),
                        pltpu.VMEM((2, nb, W_R), F32)],
        compiler_params=pltpu.CompilerParams(
            dimension_semantics=("arbitrary",), vmem_limit_bytes=_vmem_limit(est)),
        name="lru_gmlp" if tm else "lru",
    )(*args)


def _merge_kernel(x_ref, yf_ref, yb_ref, qrg_ref, gr_ref, pg_ref, gt_ref, wpr_ref, wout_ref, o_ref):
    y_lru = yf_ref[...].astype(F32) + yb_ref[...].astype(F32)
    y_r = (qrg_ref[...].astype(F32) * y_lru).astype(BF16)
    part_r = gr_ref[...].astype(F32) * jnp.dot(y_r, wpr_ref[...], preferred_element_type=F32)
    merged = (part_r + pg_ref[...].astype(F32)).astype(BF16)
    o_ref[...] = x_ref[...] + gt_ref[...] * jnp.dot(merged, wout_ref[...],
                                                    preferred_element_type=F32)


def _merge_call(x, yf, yb, qrg, gr, pg, gate, wpr, wout):
    nb, t, d = x.shape
    tm = MIX_ROWS
    tile = lambda width: pl.BlockSpec((None, tm, width), lambda b, i: (b, i, 0))
    vec = pl.BlockSpec((None, 1, d), lambda b, i: (b, 0, 0))
    est = 4 * tm * d * 4 + 2 * tm * (3 * W_R + 2 * d) * 2 + (W_R * d + d * d) * 2 + 4 * tm * W_R * 4
    return pl.pallas_call(
        _merge_kernel,
        grid=(nb, t // tm),
        in_specs=[tile(d), tile(W_R), tile(W_R), tile(W_R), tile(d), tile(d), vec,
                  _const_spec((W_R, d)), _const_spec((d, d))],
        out_specs=tile(d),
        out_shape=jax.ShapeDtypeStruct((nb, t, d), F32),
        compiler_params=pltpu.CompilerParams(
            dimension_semantics=("arbitrary", "arbitrary"), vmem_limit_bytes=_vmem_limit(est)),
        name="merge",
    )(x, yf, yb, qrg, gr, pg, gate, wpr, wout)


def _ffn_kernel(xm_ref, xp_ref, xn_ref, sh_ref, sc_ref, gt_ref, g2_ref, wg_ref, wv_ref, cw_ref,
                cb_ref, wd_ref, gf_ref, o_ref, h_s, act_s, *, tm, n_tiles):
    i = pl.program_id(1)
    rows = tm + 2 * GRID_W
    g2 = g2_ref[...]
    sc = 1.0 + sc_ref[...]
    sh = sh_ref[...]

    def modulated(x):
        return _rms(x, g2) * sc + sh

    keep_prev = jnp.where(i > 0, 1.0, 0.0)
    keep_next = jnp.where(i < n_tiles - 1, 1.0, 0.0)
    h_s[0:GRID_W, :] = (modulated(xp_ref[...]) * keep_prev).astype(BF16)
    h_s[GRID_W:GRID_W + tm, :] = modulated(xm_ref[...]).astype(BF16)
    h_s[GRID_W + tm:rows, :] = (modulated(xn_ref[...]) * keep_next).astype(BF16)

    col = lax.broadcasted_iota(jnp.int32, (rows, FFN_COLS), 0) & (GRID_W - 1)
    has_left = jnp.where(col == 0, 0.0, 1.0).astype(BF16)
    has_right = jnp.where(col == GRID_W - 1, 0.0, 1.0).astype(BF16)

    for c in range(D_FF // FFN_COLS):
        g = jnp.dot(h_s[...], wg_ref[c], preferred_element_type=F32)
        g_c = g.astype(BF16)
        g_l = pltpu.roll(g, 1, 0).astype(BF16) * has_left
        g_r = pltpu.roll(g, rows - 1, 0).astype(BF16) * has_right
        cw = cw_ref[c]
        conv = cb_ref[c]
        for dr in range(3):
            lo = GRID_W * dr
            conv = (conv + cw[3 * dr:3 * dr + 1, :] * g_l[lo:lo + tm]
                    + cw[3 * dr + 1:3 * dr + 2, :] * g_c[lo:lo + tm]
                    + cw[3 * dr + 2:3 * dr + 3, :] * g_r[lo:lo + tm])
        val = jnp.dot(h_s[GRID_W:GRID_W + tm, :], wv_ref[c], preferred_element_type=F32)
        act_s[:, c * FFN_COLS:(c + 1) * FFN_COLS] = (_gelu(conv.astype(F32)) * val).astype(BF16)
    down = jnp.dot(act_s[...], wd_ref[...], preferred_element_type=F32)
    o_ref[...] = _rms(xm_ref[...] + gt_ref[...] * down, gf_ref[...])


def _ffn_call(x1, shift, scale, gate, g2, wg3, wv3, cw3, cb3, wd3, gf):
    nb, t, d = x1.shape
    tm = FFN_ROWS
    n_tiles = t // tm
    r_h = tm // GRID_W
    n_h = t // GRID_W
    nc = D_FF // FFN_COLS
    vec = pl.BlockSpec((None, 1, d), lambda b, i: (b, 0, 0))
    est = (2 * 2 * tm * d * 4 + 4 * GRID_W * d * 4 + (2 * d * D_FF + D_FF * d) * 2
           + (tm + 2 * GRID_W) * d * 2 + tm * d * 4 + 8 * (tm + 2 * GRID_W) * FFN_COLS * 4)
    return pl.pallas_call(
        functools.partial(_ffn_kernel, tm=tm, n_tiles=n_tiles),
        grid=(nb, n_tiles),
        in_specs=[pl.BlockSpec((None, tm, d), lambda b, i: (b, i, 0)),
                  pl.BlockSpec((None, GRID_W, d), lambda b, i: (b, jnp.maximum(i * r_h - 1, 0), 0)),
                  pl.BlockSpec((None, GRID_W, d),
                               lambda b, i: (b, jnp.minimum((i + 1) * r_h, n_h - 1), 0)),
                  vec, vec, vec, _const_spec((1, d)),
                  _const_spec((nc, d, FFN_COLS)), _const_spec((nc, d, FFN_COLS)),
                  _const_spec((nc, 9, FFN_COLS)), _const_spec((nc, 1, FFN_COLS)),
                  _const_spec((D_FF, d)), _const_spec((1, d))],
        out_specs=pl.BlockSpec((None, tm, d), lambda b, i: (b, i, 0)),
        out_shape=jax.ShapeDtypeStruct((nb, t, d), F32),
        scratch_shapes=[pltpu.VMEM((tm + 2 * GRID_W, d), BF16), pltpu.VMEM((tm, D_FF), BF16)],
        compiler_params=pltpu.CompilerParams(
            dimension_semantics=("arbitrary", "arbitrary"), vmem_limit_bytes=_vmem_limit(est)),
        name="ffn",
    )(x1, x1, x1, shift, scale, gate, g2, wg3, wv3, cw3, cb3, wd3, gf)


def kernel(x, c, ctx, c_ctx, w_mod, b_mod, g_norm1, w_in, conv_w, conv_b, lru_lam, lru_wa, lru_ba,
           lru_wx, lru_bx, g_v, w_s, b_s, w_pr, w_pg, w_out, g_norm2, w_up, ffn_conv_w, ffn_conv_b,
           w_down, g_final):
    nb, t, d = x.shape
    depth = w_in.shape[0]
    assert depth == 1 and nb == V7X_SUBLANES and d == D_MODEL
    assert t % max(INPROJ_STEPS, MIX_ROWS, FFN_ROWS) == 0 and ctx.shape[1] % INPROJ_STEPS == 0
    l = 0

    pad = jnp.zeros((2 * V7X_SUBLANES - nb - 1, d), F32)
    cc = jnp.concatenate([c, c_ctx[None, :], pad], axis=0)
    m = _mod_call(cc, w_mod[l], b_mod[l][None, :]).reshape(cc.shape[0], N_MOD, d)
    m_x = m[:nb]
    m_c = jnp.broadcast_to(m[nb][None], (nb, N_MOD, d))

    w_in_b = w_in[l].astype(BF16)
    w_r, w_rest = w_in_b[:, :W_R], w_in_b[:, W_R:]
    g1 = g_norm1[l][None, :]
    wg = jnp.concatenate([lru_wa[l], lru_wx[l]], axis=-1).astype(BF16)
    bg = jnp.stack([lru_ba[l][0], lru_bx[l][0], lru_ba[l][1], lru_bx[l][1]], axis=0)
    lru_args = (conv_w[l], conv_b[l][None, :], lru_lam[l], wg, bg)

    pc = _inproj_call(ctx, m_c[:, 0], m_c[:, 1], g1, w_r)
    _, _, h_ctx = _lru_call(pc, nb, *lru_args, jnp.zeros((2, nb, W_R), F32))

    px = _inproj_call(x, m_x[:, 0], m_x[:, 1], g1, w_r)
    bs_e = jnp.broadcast_to(b_s[l].T[:, :, None], (H_G, CHUNK, GC))
    gmlp = (x, m_x[:, 0:1], m_x[:, 1:2], g1, w_rest, g_v[l][None, :], w_s[l].astype(BF16), bs_e,
            w_pg[l].astype(BF16))
    yf, yb, _, qrg, gr, pg = _lru_call(px, nb, *lru_args, h_ctx, gmlp=gmlp)
    x1 = _merge_call(x, yf, yb, qrg, gr, pg, m_x[:, 2:3], w_pr[l].astype(BF16), w_out[l].astype(BF16))

    nc = D_FF // FFN_COLS
    w_up_b = w_up[l].astype(BF16)
    wg3 = w_up_b[:, :D_FF].reshape(d, nc, FFN_COLS).transpose(1, 0, 2)
    wv3 = w_up_b[:, D_FF:].reshape(d, nc, FFN_COLS).transpose(1, 0, 2)
    cw3 = ffn_conv_w[l].reshape(9, nc, FFN_COLS).transpose(1, 0, 2).astype(BF16)
    cb3 = ffn_conv_b[l].reshape(nc, 1, FFN_COLS).astype(BF16)
    wd3 = w_down[l].astype(BF16)
    return _ffn_call(x1, m_x[:, 3:4], m_x[:, 4:5], m_x[:, 5:6], g_norm2[l][None, :],
                     wg3, wv3, cw3, cb3, wd3, g_final[None, :])
```

```python
import functools
import math

import jax
import jax.numpy as jnp
from jax import lax
from jax.experimental import pallas as pl
from jax.experimental.pallas import tpu as pltpu

D_MODEL = 1024
GRID_W = 64
W_R = 1280
H_R = 5
BW = W_R // H_R
LRU_C = 8.0
CONV_R = 4
W_G = 1024
H_G = 8
GC = W_G // H_G
CHUNK = 128
D_FF = 2816
N_MOD = 6
EPS = 1e-6
RSQRT_FLOOR = 1e-30
OFF_RG = W_R
OFF_U = OFF_RG + W_R
OFF_V = OFF_U + W_G
OFF_GR = OFF_V + W_G
OFF_GG = OFF_GR + D_MODEL
N_IN = OFF_GG + D_MODEL

V7X_LANES = 128
V7X_SUBLANES = 8
V7X_MXU_WIDTH = 256
V7X_VMEM_BYTES = 64 * 1024 * 1024

F32 = jnp.float32
BF16 = jnp.bfloat16

LRU_STEPS = 64
MIX_ROWS = 512
FFN_ROWS = 512
FFN_COLS = V7X_MXU_WIDTH
MOD_COLS = 1024
LOOKAHEAD = 2


def _vmem_limit(nbytes):
    return int(min(max(2 * nbytes, 16 * 1024 * 1024), V7X_VMEM_BYTES - 8 * 1024 * 1024))


def _gelu(x):
    k = math.sqrt(2.0 / math.pi)
    hx = 0.5 * x
    return hx + hx * jnp.tanh(x * (k + (k * 0.044715) * (x * x)))


def _sigmoid(x):
    return 1.0 / (1.0 + jnp.exp2(x * (-math.log2(math.e))))


def _rms(x, g):
    return x * lax.rsqrt(jnp.mean(x * x, axis=-1, keepdims=True) + EPS) * g


def _const_spec(shape):
    nd = len(shape)
    return pl.BlockSpec(shape, lambda *_: (0,) * nd, pipeline_mode=pl.Buffered(1))


def _mod_kernel(c_ref, w_ref, b_ref, o_ref):
    c = c_ref[...]
    s = c * _sigmoid(c)
    o_ref[...] = jnp.dot(s, w_ref[...], preferred_element_type=F32,
                         precision=lax.Precision.HIGHEST) + b_ref[...]


def _mod_call(cc, w, b):
    rows, d = cc.shape
    n = w.shape[1]
    return pl.pallas_call(
        _mod_kernel,
        grid=(n // MOD_COLS,),
        in_specs=[pl.BlockSpec((rows, d), lambda j: (0, 0)),
                  pl.BlockSpec((d, MOD_COLS), lambda j: (0, j)),
                  pl.BlockSpec((1, MOD_COLS), lambda j: (0, j))],
        out_specs=pl.BlockSpec((rows, MOD_COLS), lambda j: (0, j)),
        out_shape=jax.ShapeDtypeStruct((rows, n), F32),
        compiler_params=pltpu.CompilerParams(
            dimension_semantics=("arbitrary",),
            vmem_limit_bytes=_vmem_limit(2 * d * MOD_COLS * 4)),
        name="mod",
    )(cc, w, b)


def _inproj_kernel(x_ref, xn_ref, sh_ref, sc_ref, g_ref, w_ref, cw_ref, cb_ref, o_ref, hs_ref, tail_ref,
                   *, steps, n_tiles):
    i = pl.program_id(0)
    nb = x_ref.shape[0]
    rows = steps * nb
    n_slab = D_MODEL // V7X_LANES
    g = g_ref[...]

    @pl.when(i == 0)
    def _():
        tail_ref[...] = jnp.zeros_like(tail_ref)

    half = steps // 2
    parts = []
    for part in range(2):
        t0 = part * half
        for b in range(nb):
            sc = 1.0 + sc_ref[b:b + 1, :]
            sh = sh_ref[b:b + 1, :]
            hb = _rms(x_ref[b, t0:t0 + half, :], g) * sc + sh
            for s in range(n_slab):
                lanes = slice(s * V7X_LANES, (s + 1) * V7X_LANES)
                hs_ref[s, pl.ds(t0 * nb + b, half, stride=nb), :] = hb[:, lanes]
            if part == 1:
                hn = _rms(xn_ref[b], g)[0:LOOKAHEAD] * sc + sh
                for s in range(n_slab):
                    lanes = slice(s * V7X_LANES, (s + 1) * V7X_LANES)
                    hs_ref[s, pl.ds(rows + b, LOOKAHEAD, stride=nb), :] = hn[:, lanes]
        lo = t0 * nb
        hi = lo + half * nb + (LOOKAHEAD * nb if part == 1 else 0)
        h = jnp.concatenate([hs_ref[s, lo:hi, :] for s in range(n_slab)], axis=1).astype(BF16)
        parts.append(jnp.dot(h, w_ref[...], preferred_element_type=F32))
    p = jnp.concatenate(parts, axis=0)

    keep_next = jnp.where(i < n_tiles - 1, 1.0, 0.0)
    pe = jnp.concatenate([tail_ref[...], p[0:rows], p[rows:rows + nb] * keep_next], axis=0)
    xr = cb_ref[...]
    for k in range(CONV_R):
        xr = xr + cw_ref[k:k + 1, :] * pe[k * nb:k * nb + rows]
    o_ref[...] = xr
    tail_ref[...] = p[rows - (CONV_R - 2) * nb:rows]


def _inproj_call(x, shift, scale, g, w_r, cw, cb):
    nb, t, d = x.shape
    steps = LRU_STEPS
    rows = steps * nb
    n_tiles = t // steps
    look = V7X_SUBLANES
    r_l = steps // look
    n_l = t // look
    est = (2 * nb * (steps + look) * d * 4 + d * W_R * 2 + 2 * rows * W_R * 4
           + (rows + LOOKAHEAD * nb) * (d + 2 * W_R) * 4)
    return pl.pallas_call(
        functools.partial(_inproj_kernel, steps=steps, n_tiles=n_tiles),
        grid=(n_tiles,),
        in_specs=[pl.BlockSpec((nb, steps, d), lambda i: (0, i, 0)),
                  pl.BlockSpec((nb, look, d), lambda i: (0, jnp.minimum((i + 1) * r_l, n_l - 1), 0)),
                  _const_spec((nb, d)), _const_spec((nb, d)), _const_spec((1, d)),
                  _const_spec((d, W_R)), _const_spec((CONV_R, W_R)), _const_spec((1, W_R))],
        out_specs=pl.BlockSpec((rows, W_R), lambda i: (i, 0)),
        out_shape=jax.ShapeDtypeStruct((t * nb, W_R), F32),
        scratch_shapes=[pltpu.VMEM((d // V7X_LANES, rows + LOOKAHEAD * nb, V7X_LANES), F32),
                        pltpu.VMEM(((CONV_R - 2) * nb, W_R), F32)],
        compiler_params=pltpu.CompilerParams(
            dimension_semantics=("arbitrary",), vmem_limit_bytes=_vmem_limit(est)),
        name="inproj",
    )(x, x, shift, scale, g, w_r, cw, cb)


def _lru_kernel(xf_ref, xb_ref, lam_ref, wg_ref, bg_ref, h0_ref,
                yf_ref, yb_ref, hl_ref, a_s, b_s, y_s, hc, *, steps):
    nb = V7X_SUBLANES
    n_slab = W_R // V7X_LANES

    @pl.when(pl.program_id(0) == 0)
    def _():
        hc[...] = h0_ref[...]

    z = -lam_ref[...]
    softplus = jnp.maximum(z, 0.0) + jnp.log1p(jnp.exp(-jnp.abs(z)))
    decay = (-LRU_C * math.log2(math.e)) * softplus

    def coefficients(d, x_ref, h):
        cs = slice(h * BW, (h + 1) * BW)
        xr = x_ref[:, cs]
        gates = jnp.dot(xr.astype(BF16), wg_ref[d, h], preferred_element_type=F32)
        r = _sigmoid(gates[:, :BW] + bg_ref[2 * d:2 * d + 1, cs])
        ig = _sigmoid(gates[:, BW:] + bg_ref[2 * d + 1:2 * d + 2, cs])
        a = jnp.exp2(r * decay[d:d + 1, cs])
        a_s[d, :, cs] = a
        y = 1.0 - a * a
        b_s[d, :, cs] = (y * lax.rsqrt(jnp.maximum(y, RSQRT_FLOOR))) * (ig * xr)

    slabs_per_head = BW // V7X_LANES
    for h in range(H_R):
        cs = slice(h * BW, (h + 1) * BW)
        coefficients(0, xf_ref, h)
        coefficients(1, xb_ref, h)
        hf = hc[0, :, cs]
        hb = hc[1, :, cs]
        for t in range(steps):
            rf = slice(t * nb, (t + 1) * nb)
            rb = slice((steps - 1 - t) * nb, (steps - t) * nb)
            hf = a_s[0, rf, cs] * hf + b_s[0, rf, cs]
            hb = a_s[1, rb, cs] * hb + b_s[1, rb, cs]
            for k in range(slabs_per_head):
                lanes = slice(k * V7X_LANES, (k + 1) * V7X_LANES)
                y_s[0, h * slabs_per_head + k, rf, :] = hf[:, lanes]
                y_s[1, h * slabs_per_head + k, rb, :] = hb[:, lanes]
        hc[0, :, cs] = hf
        hc[1, :, cs] = hb
        hl_ref[0, :, cs] = hf
        hl_ref[1, :, cs] = hb

    for b in range(nb):
        yf_ref[b] = jnp.concatenate(
            [y_s[0, s, pl.ds(b, steps, stride=nb), :] for s in range(n_slab)], axis=1).astype(BF16)
        yb_ref[b] = jnp.concatenate(
            [y_s[1, s, pl.ds(b, steps, stride=nb), :] for s in range(n_slab)], axis=1).astype(BF16)


def _lru_call(xr, nb, lam, wg, bg, h0):
    steps = LRU_STEPS
    rows = steps * nb
    t = xr.shape[0] // nb
    n_tiles = t // steps
    est = (2 * 2 * rows * W_R * 4 + 2 * 2 * nb * steps * W_R * 2 + 3 * 2 * rows * W_R * 4
           + 2 * H_R * BW * 2 * BW * 2 + 4 * rows * W_R * 4)
    return pl.pallas_call(
        functools.partial(_lru_kernel, steps=steps),
        grid=(n_tiles,),
        in_specs=[pl.BlockSpec((rows, W_R), lambda i: (i, 0)),
                  pl.BlockSpec((rows, W_R), lambda i: (n_tiles - 1 - i, 0)),
                  _const_spec((2, W_R)), _const_spec((2, H_R, BW, 2 * BW)), _const_spec((4, W_R)),
                  _const_spec((2, nb, W_R))],
        out_specs=[pl.BlockSpec((nb, steps, W_R), lambda i: (0, i, 0)),
                   pl.BlockSpec((nb, steps, W_R), lambda i: (0, n_tiles - 1 - i, 0)),
                   pl.BlockSpec((2, nb, W_R), lambda i: (0, 0, 0))],
        out_shape=[jax.ShapeDtypeStruct((nb, t, W_R), BF16),
                   jax.ShapeDtypeStruct((nb, t, W_R), BF16),
                   jax.ShapeDtypeStruct((2, nb, W_R), F32)],
        scratch_shapes=[pltpu.VMEM((2, rows, W_R), F32),
                        pltpu.VMEM((2, rows, W_R), F32),
                        pltpu.VMEM((2, W_R // V7X_LANES, rows, V7X_LANES), F32),
                        pltpu.VMEM((2, nb, W_R), F32)],
        compiler_params=pltpu.CompilerParams(
            dimension_semantics=("arbitrary",), vmem_limit_bytes=_vmem_limit(est)),
        name="lru",
    )(xr, xr, lam, wg, bg, h0)


def _mixer_kernel(x_ref, yf_ref, yb_ref, sh_ref, sc_ref, gt_ref, g1_ref, win_ref, gv_ref,
                  ws_ref, bs_ref, wpr_ref, wpg_ref, wout_ref, o_ref, *, tm):
    x = x_ref[...]
    h = (_rms(x, g1_ref[...]) * (1.0 + sc_ref[...]) + sh_ref[...]).astype(BF16)

    def proj(off, width):
        lo = off - W_R
        return jnp.dot(h, win_ref[:, lo:lo + width], preferred_element_type=F32)

    y_lru = yf_ref[...].astype(F32) + yb_ref[...].astype(F32)
    y_r = (_gelu(proj(OFF_RG, W_R)) * y_lru).astype(BF16)
    part_r = _sigmoid(proj(OFF_GR, D_MODEL)) * jnp.dot(y_r, wpr_ref[...], preferred_element_type=F32)

    v = _rms(_gelu(proj(OFF_V, W_G)), gv_ref[...]).astype(BF16)
    chunks = []
    for n in range(tm // CHUNK):
        heads = []
        for hh in range(H_G):
            blk = v[n * CHUNK:(n + 1) * CHUNK, hh * GC:(hh + 1) * GC]
            heads.append(jnp.dot(ws_ref[hh], blk, preferred_element_type=F32) + bs_ref[hh])
        chunks.append(jnp.concatenate(heads, axis=1))
    s = jnp.concatenate(chunks, axis=0)
    y_g = (_gelu(proj(OFF_U, W_G)) * s).astype(BF16)
    part_g = _sigmoid(proj(OFF_GG, D_MODEL)) * jnp.dot(y_g, wpg_ref[...], preferred_element_type=F32)

    merged = (part_r + part_g).astype(BF16)
    o_ref[...] = x + gt_ref[...] * jnp.dot(merged, wout_ref[...], preferred_element_type=F32)


def _mixer_call(x, yf, yb, shift, scale, gate, g1, w_rest, gv, ws, bs, wpr, wpg, wout):
    nb, t, d = x.shape
    tm = MIX_ROWS
    n_rest = N_IN - W_R
    tile = lambda width: pl.BlockSpec((None, tm, width), lambda b, i: (b, i, 0))
    vec = pl.BlockSpec((None, 1, d), lambda b, i: (b, 0, 0))
    est = (4 * tm * d * 4 + 4 * tm * W_R * 2
           + (d * n_rest + W_R * d + 2 * d * d + H_G * CHUNK * CHUNK) * 2 + H_G * CHUNK * GC * 4
           + 6 * tm * W_R * 4)
    return pl.pallas_call(
        functools.partial(_mixer_kernel, tm=tm),
        grid=(nb, t // tm),
        in_specs=[tile(d), tile(W_R), tile(W_R), vec, vec, vec,
                  _const_spec((1, d)), _const_spec((d, n_rest)), _const_spec((1, W_G)),
                  _const_spec((H_G, CHUNK, CHUNK)), _const_spec((H_G, CHUNK, GC)),
                  _const_spec((W_R, d)), _const_spec((W_G, d)), _const_spec((d, d))],
        out_specs=tile(d),
        out_shape=jax.ShapeDtypeStruct((nb, t, d), F32),
        compiler_params=pltpu.CompilerParams(
            dimension_semantics=("arbitrary", "arbitrary"), vmem_limit_bytes=_vmem_limit(est)),
        name="mixer",
    )(x, yf, yb, shift, scale, gate, g1, w_rest, gv, ws, bs, wpr, wpg, wout)


def _ffn_kernel(xm_ref, xp_ref, xn_ref, sh_ref, sc_ref, gt_ref, g2_ref, wg_ref, wv_ref, cw_ref,
                cb_ref, wd_ref, gf_ref, o_ref, h_s, act_s, *, tm, n_tiles):
    i = pl.program_id(1)
    rows = tm + 2 * GRID_W
    g2 = g2_ref[...]
    sc = 1.0 + sc_ref[...]
    sh = sh_ref[...]

    def modulated(x):
        return _rms(x, g2) * sc + sh

    keep_prev = jnp.where(i > 0, 1.0, 0.0)
    keep_next = jnp.where(i < n_tiles - 1, 1.0, 0.0)
    h_s[0:GRID_W, :] = (modulated(xp_ref[...]) * keep_prev).astype(BF16)
    h_s[GRID_W:GRID_W + tm, :] = modulated(xm_ref[...]).astype(BF16)
    h_s[GRID_W + tm:rows, :] = (modulated(xn_ref[...]) * keep_next).astype(BF16)

    col = lax.broadcasted_iota(jnp.int32, (rows, FFN_COLS), 0) & (GRID_W - 1)
    has_left = jnp.where(col == 0, 0.0, 1.0).astype(BF16)
    has_right = jnp.where(col == GRID_W - 1, 0.0, 1.0).astype(BF16)

    for c in range(D_FF // FFN_COLS):
        g = jnp.dot(h_s[...], wg_ref[c], preferred_element_type=F32)
        g_c = g.astype(BF16)
        g_l = pltpu.roll(g, 1, 0).astype(BF16) * has_left
        g_r = pltpu.roll(g, rows - 1, 0).astype(BF16) * has_right
        cw = cw_ref[c]
        conv = cb_ref[c]
        for dr in range(3):
            lo = GRID_W * dr
            conv = (conv + cw[3 * dr:3 * dr + 1, :] * g_l[lo:lo + tm]
                    + cw[3 * dr + 1:3 * dr + 2, :] * g_c[lo:lo + tm]
                    + cw[3 * dr + 2:3 * dr + 3, :] * g_r[lo:lo + tm])
        val = jnp.dot(h_s[GRID_W:GRID_W + tm, :], wv_ref[c], preferred_element_type=F32)
        act_s[:, c * FFN_COLS:(c + 1) * FFN_COLS] = (_gelu(conv.astype(F32)) * val).astype(BF16)
    down = jnp.dot(act_s[...], wd_ref[...], preferred_element_type=F32)
    o_ref[...] = _rms(xm_ref[...] + gt_ref[...] * down, gf_ref[...])


def _ffn_call(x1, shift, scale, gate, g2, wg3, wv3, cw3, cb3, wd3, gf):
    nb, t, d = x1.shape
    tm = FFN_ROWS
    n_tiles = t // tm
    r_h = tm // GRID_W
    n_h = t // GRID_W
    nc = D_FF // FFN_COLS
    vec = pl.BlockSpec((None, 1, d), lambda b, i: (b, 0, 0))
    est = (2 * 2 * tm * d * 4 + 4 * GRID_W * d * 4 + (2 * d * D_FF + D_FF * d) * 2
           + (tm + 2 * GRID_W) * d * 2 + tm * d * 4 + 8 * (tm + 2 * GRID_W) * FFN_COLS * 4)
    return pl.pallas_call(
        functools.partial(_ffn_kernel, tm=tm, n_tiles=n_tiles),
        grid=(nb, n_tiles),
        in_specs=[pl.BlockSpec((None, tm, d), lambda b, i: (b, i, 0)),
                  pl.BlockSpec((None, GRID_W, d), lambda b, i: (b, jnp.maximum(i * r_h - 1, 0), 0)),
                  pl.BlockSpec((None, GRID_W, d),
                               lambda b, i: (b, jnp.minimum((i + 1) * r_h, n_h - 1), 0)),
                  vec, vec, vec, _const_spec((1, d)),
                  _const_spec((nc, d, FFN_COLS)), _const_spec((nc, d, FFN_COLS)),
                  _const_spec((nc, 9, FFN_COLS)), _const_spec((nc, 1, FFN_COLS)),
                  _const_spec((D_FF, d)), _const_spec((1, d))],
        out_specs=pl.BlockSpec((None, tm, d), lambda b, i: (b, i, 0)),
        out_shape=jax.ShapeDtypeStruct((nb, t, d), F32),
        scratch_shapes=[pltpu.VMEM((tm + 2 * GRID_W, d), BF16), pltpu.VMEM((tm, D_FF), BF16)],
        compiler_params=pltpu.CompilerParams(
            dimension_semantics=("arbitrary", "arbitrary"), vmem_limit_bytes=_vmem_limit(est)),
        name="ffn",
    )(x1, x1, x1, shift, scale, gate, g2, wg3, wv3, cw3, cb3, wd3, gf)


def kernel(x, c, ctx, c_ctx, w_mod, b_mod, g_norm1, w_in, conv_w, conv_b, lru_lam, lru_wa, lru_ba,
           lru_wx, lru_bx, g_v, w_s, b_s, w_pr, w_pg, w_out, g_norm2, w_up, ffn_conv_w, ffn_conv_b,
           w_down, g_final):
    nb, t, d = x.shape
    depth = w_in.shape[0]
    assert depth == 1 and nb == V7X_SUBLANES and d == D_MODEL
    assert t % max(LRU_STEPS, MIX_ROWS, FFN_ROWS) == 0 and ctx.shape[1] % LRU_STEPS == 0
    l = 0

    pad = jnp.zeros((2 * V7X_SUBLANES - nb - 1, d), F32)
    cc = jnp.concatenate([c, c_ctx[None, :], pad], axis=0)
    m = _mod_call(cc, w_mod[l], b_mod[l][None, :]).reshape(cc.shape[0], N_MOD, d)
    m_x = m[:nb]
    m_c = jnp.broadcast_to(m[nb][None], (nb, N_MOD, d))

    w_in_b = w_in[l].astype(BF16)
    w_r, w_rest = w_in_b[:, :W_R], w_in_b[:, W_R:]
    g1 = g_norm1[l][None, :]
    wg = jnp.concatenate([lru_wa[l], lru_wx[l]], axis=-1).astype(BF16)
    bg = jnp.stack([lru_ba[l][0], lru_bx[l][0], lru_ba[l][1], lru_bx[l][1]], axis=0)
    conv_args = (conv_w[l], conv_b[l][None, :])
    lru_args = (lru_lam[l], wg, bg)

    pc = _inproj_call(ctx, m_c[:, 0], m_c[:, 1], g1, w_r, *conv_args)
    _, _, h_ctx = _lru_call(pc, nb, *lru_args, jnp.zeros((2, nb, W_R), F32))

    px = _inproj_call(x, m_x[:, 0], m_x[:, 1], g1, w_r, *conv_args)
    yf, yb, _ = _lru_call(px, nb, *lru_args, h_ctx)

    bs_e = jnp.broadcast_to(b_s[l].T[:, :, None], (H_G, CHUNK, GC))
    x1 = _mixer_call(x, yf, yb, m_x[:, 0:1], m_x[:, 1:2], m_x[:, 2:3], g1, w_rest,
                     g_v[l][None, :], w_s[l].astype(BF16), bs_e,
                     w_pr[l].astype(BF16), w_pg[l].astype(BF16), w_out[l].astype(BF16))

    nc = D_FF // FFN_COLS
    w_up_b = w_up[l].astype(BF16)
    wg3 = w_up_b[:, :D_FF].reshape(d, nc, FFN_COLS).transpose(1, 0, 2)
    wv3 = w_up_b[:, D_FF:].reshape(d, nc, FFN_COLS).transpose(1, 0, 2)
    cw3 = ffn_conv_w[l].reshape(9, nc, FFN_COLS).transpose(1, 0, 2).astype(BF16)
    cb3 = ffn_conv_b[l].reshape(nc, 1, FFN_COLS).astype(BF16)
    wd3 = w_down[l].astype(BF16)
    return _ffn_call(x1, m_x[:, 3:4], m_x[:, 4:5], m_x[:, 5:6], g_norm2[l][None, :],
                     wg3, wv3, cw3, cb3, wd3, g_final[None, :])
```

```python
import functools
import math

import jax
import jax.numpy as jnp
from jax import lax
from jax.experimental import pallas as pl
from jax.experimental.pallas import tpu as pltpu

D_MODEL = 1024
GRID_W = 64
W_R = 1280
H_R = 5
BW = W_R // H_R
LRU_C = 8.0
CONV_R = 4
W_G = 1024
H_G = 8
GC = W_G // H_G
CHUNK = 128
D_FF = 2816
N_MOD = 6
EPS = 1e-6
RSQRT_FLOOR = 1e-30
OFF_RG = W_R
OFF_U = OFF_RG + W_R
OFF_V = OFF_U + W_G
OFF_GR = OFF_V + W_G
OFF_GG = OFF_GR + D_MODEL
N_IN = OFF_GG + D_MODEL

V7X_LANES = 128
V7X_SUBLANES = 8
V7X_MXU_WIDTH = 256
V7X_VMEM_BYTES = 64 * 1024 * 1024

F32 = jnp.float32
BF16 = jnp.bfloat16

LRU_STEPS = 64
MIX_ROWS = 512
FFN_ROWS = 512
FFN_COLS = V7X_MXU_WIDTH
MOD_COLS = 1024
LOOKAHEAD = 2


def _vmem_limit(nbytes):
    return int(min(max(2 * nbytes, 16 * 1024 * 1024), V7X_VMEM_BYTES - 8 * 1024 * 1024))


def _gelu(x):
    k = math.sqrt(2.0 / math.pi)
    hx = 0.5 * x
    return hx + hx * jnp.tanh(x * (k + (k * 0.044715) * (x * x)))


def _sigmoid(x):
    return 1.0 / (1.0 + jnp.exp2(x * (-math.log2(math.e))))


def _rms(x, g):
    return x * lax.rsqrt(jnp.mean(x * x, axis=-1, keepdims=True) + EPS) * g


def _const_spec(shape):
    nd = len(shape)
    return pl.BlockSpec(shape, lambda *_: (0,) * nd, pipeline_mode=pl.Buffered(1))


def _mod_kernel(c_ref, w_ref, b_ref, o_ref):
    c = c_ref[...]
    s = c * _sigmoid(c)
    o_ref[...] = jnp.dot(s, w_ref[...], preferred_element_type=F32,
                         precision=lax.Precision.HIGHEST) + b_ref[...]


def _mod_call(cc, w, b):
    rows, d = cc.shape
    n = w.shape[1]
    return pl.pallas_call(
        _mod_kernel,
        grid=(n // MOD_COLS,),
        in_specs=[pl.BlockSpec((rows, d), lambda j: (0, 0)),
                  pl.BlockSpec((d, MOD_COLS), lambda j: (0, j)),
                  pl.BlockSpec((1, MOD_COLS), lambda j: (0, j))],
        out_specs=pl.BlockSpec((rows, MOD_COLS), lambda j: (0, j)),
        out_shape=jax.ShapeDtypeStruct((rows, n), F32),
        compiler_params=pltpu.CompilerParams(
            dimension_semantics=("arbitrary",),
            vmem_limit_bytes=_vmem_limit(2 * d * MOD_COLS * 4)),
        name="mod",
    )(cc, w, b)


def _inproj_kernel(x_ref, xn_ref, sh_ref, sc_ref, g_ref, w_ref, cw_ref, cb_ref, o_ref, hs_ref, tail_ref,
                   *, steps, n_tiles):
    i = pl.program_id(0)
    nb = x_ref.shape[0]
    rows = steps * nb
    n_slab = D_MODEL // V7X_LANES
    g = g_ref[...]

    @pl.when(i == 0)
    def _():
        tail_ref[...] = jnp.zeros_like(tail_ref)

    half = steps // 2
    parts = []
    for part in range(2):
        t0 = part * half
        for b in range(nb):
            sc = 1.0 + sc_ref[b:b + 1, :]
            sh = sh_ref[b:b + 1, :]
            hb = _rms(x_ref[b, t0:t0 + half, :], g) * sc + sh
            for s in range(n_slab):
                lanes = slice(s * V7X_LANES, (s + 1) * V7X_LANES)
                hs_ref[s, pl.ds(t0 * nb + b, half, stride=nb), :] = hb[:, lanes]
            if part == 1:
                hn = _rms(xn_ref[b], g)[0:LOOKAHEAD] * sc + sh
                for s in range(n_slab):
                    lanes = slice(s * V7X_LANES, (s + 1) * V7X_LANES)
                    hs_ref[s, pl.ds(rows + b, LOOKAHEAD, stride=nb), :] = hn[:, lanes]
        lo = t0 * nb
        hi = lo + half * nb + (LOOKAHEAD * nb if part == 1 else 0)
        h = jnp.concatenate([hs_ref[s, lo:hi, :] for s in range(n_slab)], axis=1).astype(BF16)
        parts.append(jnp.dot(h, w_ref[...], preferred_element_type=F32))
    p = jnp.concatenate(parts, axis=0)

    keep_next = jnp.where(i < n_tiles - 1, 1.0, 0.0)
    pe = jnp.concatenate([tail_ref[...], p[0:rows], p[rows:rows + nb] * keep_next], axis=0)
    xr = cb_ref[...]
    for k in range(CONV_R):
        xr = xr + cw_ref[k:k + 1, :] * pe[k * nb:k * nb + rows]
    o_ref[...] = xr
    tail_ref[...] = p[rows - (CONV_R - 2) * nb:rows]


def _inproj_call(x, shift, scale, g, w_r, cw, cb):
    nb, t, d = x.shape
    steps = LRU_STEPS
    rows = steps * nb
    n_tiles = t // steps
    look = V7X_SUBLANES
    r_l = steps // look
    n_l = t // look
    est = (2 * nb * (steps + look) * d * 4 + d * W_R * 2 + 2 * rows * W_R * 4
           + (rows + LOOKAHEAD * nb) * (d + 2 * W_R) * 4)
    return pl.pallas_call(
        functools.partial(_inproj_kernel, steps=steps, n_tiles=n_tiles),
        grid=(n_tiles,),
        in_specs=[pl.BlockSpec((nb, steps, d), lambda i: (0, i, 0)),
                  pl.BlockSpec((nb, look, d), lambda i: (0, jnp.minimum((i + 1) * r_l, n_l - 1), 0)),
                  _const_spec((nb, d)), _const_spec((nb, d)), _const_spec((1, d)),
                  _const_spec((d, W_R)), _const_spec((CONV_R, W_R)), _const_spec((1, W_R))],
        out_specs=pl.BlockSpec((rows, W_R), lambda i: (i, 0)),
        out_shape=jax.ShapeDtypeStruct((t * nb, W_R), F32),
        scratch_shapes=[pltpu.VMEM((d // V7X_LANES, rows + LOOKAHEAD * nb, V7X_LANES), F32),
                        pltpu.VMEM(((CONV_R - 2) * nb, W_R), F32)],
        compiler_params=pltpu.CompilerParams(
            dimension_semantics=("arbitrary",), vmem_limit_bytes=_vmem_limit(est)),
        name="inproj",
    )(x, x, shift, scale, g, w_r, cw, cb)


def _lru_kernel(xf_ref, xb_ref, lam_ref, wg_ref, bg_ref, h0_ref,
                yf_ref, yb_ref, hl_ref, a_s, b_s, y_s, hc, *, steps):
    nb = V7X_SUBLANES
    n_slab = W_R // V7X_LANES

    @pl.when(pl.program_id(0) == 0)
    def _():
        hc[...] = h0_ref[...]

    z = -lam_ref[...]
    softplus = jnp.maximum(z, 0.0) + jnp.log1p(jnp.exp(-jnp.abs(z)))
    half_decay = (-0.5 * LRU_C * math.log2(math.e)) * softplus

    def coefficients(d, x_ref, h):
        cs = slice(h * BW, (h + 1) * BW)
        xr = x_ref[:, cs]
        half_gates = jnp.dot(xr.astype(BF16), wg_ref[d, h], preferred_element_type=F32)
        th_r = jnp.tanh(half_gates[:, :BW] + bg_ref[2 * d:2 * d + 1, cs])
        th_i = jnp.tanh(half_gates[:, BW:] + bg_ref[2 * d + 1:2 * d + 2, cs])
        a = jnp.exp2(half_decay[d:d + 1, cs] + th_r * half_decay[d:d + 1, cs])
        a_s[d, :, cs] = a
        y = 1.0 - a * a
        gated_x = (0.5 * xr) * (1.0 + th_i)
        b_s[d, :, cs] = (y * lax.rsqrt(jnp.maximum(y, RSQRT_FLOOR))) * gated_x

    slabs_per_head = BW // V7X_LANES
    for h in range(H_R):
        cs = slice(h * BW, (h + 1) * BW)
        coefficients(0, xf_ref, h)
        coefficients(1, xb_ref, h)
        hf = hc[0, :, cs]
        hb = hc[1, :, cs]
        for t in range(steps):
            rf = slice(t * nb, (t + 1) * nb)
            rb = slice((steps - 1 - t) * nb, (steps - t) * nb)
            hf = a_s[0, rf, cs] * hf + b_s[0, rf, cs]
            hb = a_s[1, rb, cs] * hb + b_s[1, rb, cs]
            for k in range(slabs_per_head):
                lanes = slice(k * V7X_LANES, (k + 1) * V7X_LANES)
                y_s[0, h * slabs_per_head + k, rf, :] = hf[:, lanes]
                y_s[1, h * slabs_per_head + k, rb, :] = hb[:, lanes]
        hc[0, :, cs] = hf
        hc[1, :, cs] = hb
        hl_ref[0, :, cs] = hf
        hl_ref[1, :, cs] = hb

    for b in range(nb):
        yf_ref[b] = jnp.concatenate(
            [y_s[0, s, pl.ds(b, steps, stride=nb), :] for s in range(n_slab)], axis=1).astype(BF16)
        yb_ref[b] = jnp.concatenate(
            [y_s[1, s, pl.ds(b, steps, stride=nb), :] for s in range(n_slab)], axis=1).astype(BF16)


def _lru_call(xr, nb, lam, wg, bg, h0):
    steps = LRU_STEPS
    rows = steps * nb
    t = xr.shape[0] // nb
    n_tiles = t // steps
    est = (2 * 2 * rows * W_R * 4 + 2 * 2 * nb * steps * W_R * 2 + 3 * 2 * rows * W_R * 4
           + 2 * H_R * BW * 2 * BW * 2 + 4 * rows * W_R * 4)
    return pl.pallas_call(
        functools.partial(_lru_kernel, steps=steps),
        grid=(n_tiles,),
        in_specs=[pl.BlockSpec((rows, W_R), lambda i: (i, 0)),
                  pl.BlockSpec((rows, W_R), lambda i: (n_tiles - 1 - i, 0)),
                  _const_spec((2, W_R)), _const_spec((2, H_R, BW, 2 * BW)), _const_spec((4, W_R)),
                  _const_spec((2, nb, W_R))],
        out_specs=[pl.BlockSpec((nb, steps, W_R), lambda i: (0, i, 0)),
                   pl.BlockSpec((nb, steps, W_R), lambda i: (0, n_tiles - 1 - i, 0)),
                   pl.BlockSpec((2, nb, W_R), lambda i: (0, 0, 0))],
        out_shape=[jax.ShapeDtypeStruct((nb, t, W_R), BF16),
                   jax.ShapeDtypeStruct((nb, t, W_R), BF16),
                   jax.ShapeDtypeStruct((2, nb, W_R), F32)],
        scratch_shapes=[pltpu.VMEM((2, rows, W_R), F32),
                        pltpu.VMEM((2, rows, W_R), F32),
                        pltpu.VMEM((2, W_R // V7X_LANES, rows, V7X_LANES), F32),
                        pltpu.VMEM((2, nb, W_R), F32)],
        compiler_params=pltpu.CompilerParams(
            dimension_semantics=("arbitrary",), vmem_limit_bytes=_vmem_limit(est)),
        name="lru",
    )(xr, xr, lam, wg, bg, h0)


def _mixer_kernel(x_ref, yf_ref, yb_ref, sh_ref, sc_ref, gt_ref, g1_ref, win_ref, gv_ref,
                  ws_ref, bs_ref, wpr_ref, wpg_ref, wout_ref, o_ref, *, tm):
    w = V7X_MXU_WIDTH
    x = x_ref[...]
    h = (_rms(x, g1_ref[...]) * (1.0 + sc_ref[...]) + sh_ref[...]).astype(BF16)

    def cols(j):
        return slice(j * w, (j + 1) * w)

    def proj(off, j):
        lo = off - W_R + j * w
        return jnp.dot(h, win_ref[:, lo:lo + w], preferred_element_type=F32)

    v = jnp.concatenate([_gelu(proj(OFF_V, j)) for j in range(W_G // w)], axis=1)
    v = _rms(v, gv_ref[...]).astype(BF16)

    y_r = jnp.concatenate(
        [(_gelu(proj(OFF_RG, j)) * (yf_ref[:, cols(j)].astype(F32) + yb_ref[:, cols(j)].astype(F32))
          ).astype(BF16) for j in range(W_R // w)], axis=1)

    chunks = []
    for n in range(tm // CHUNK):
        heads = []
        for hh in range(H_G):
            blk = v[n * CHUNK:(n + 1) * CHUNK, hh * GC:(hh + 1) * GC]
            heads.append(jnp.dot(ws_ref[hh], blk, preferred_element_type=F32) + bs_ref[hh])
        chunks.append(jnp.concatenate(heads, axis=1))
    s = jnp.concatenate(chunks, axis=0)
    y_g = jnp.concatenate(
        [(_gelu(proj(OFF_U, j)) * s[:, cols(j)]).astype(BF16) for j in range(W_G // w)], axis=1)

    merged = []
    for j in range(D_MODEL // w):
        part_r = _sigmoid(proj(OFF_GR, j)) * jnp.dot(y_r, wpr_ref[:, cols(j)],
                                                     preferred_element_type=F32)
        part_g = _sigmoid(proj(OFF_GG, j)) * jnp.dot(y_g, wpg_ref[:, cols(j)],
                                                     preferred_element_type=F32)
        merged.append((part_r + part_g).astype(BF16))
    merged = jnp.concatenate(merged, axis=1)
    o_ref[...] = x + gt_ref[...] * jnp.dot(merged, wout_ref[...], preferred_element_type=F32)


def _mixer_call(x, yf, yb, shift, scale, gate, g1, w_rest, gv, ws, bs, wpr, wpg, wout):
    nb, t, d = x.shape
    tm = MIX_ROWS
    n_rest = N_IN - W_R
    tile = lambda width: pl.BlockSpec((None, tm, width), lambda b, i: (b, i, 0))
    vec = pl.BlockSpec((None, 1, d), lambda b, i: (b, 0, 0))
    est = (4 * tm * d * 4 + 4 * tm * W_R * 2
           + (d * n_rest + W_R * d + 2 * d * d + H_G * CHUNK * CHUNK) * 2 + H_G * CHUNK * GC * 4
           + 6 * tm * W_R * 4)
    return pl.pallas_call(
        functools.partial(_mixer_kernel, tm=tm),
        grid=(nb, t // tm),
        in_specs=[tile(d), tile(W_R), tile(W_R), vec, vec, vec,
                  _const_spec((1, d)), _const_spec((d, n_rest)), _const_spec((1, W_G)),
                  _const_spec((H_G, CHUNK, CHUNK)), _const_spec((H_G, CHUNK, GC)),
                  _const_spec((W_R, d)), _const_spec((W_G, d)), _const_spec((d, d))],
        out_specs=tile(d),
        out_shape=jax.ShapeDtypeStruct((nb, t, d), F32),
        compiler_params=pltpu.CompilerParams(
            dimension_semantics=("arbitrary", "arbitrary"), vmem_limit_bytes=_vmem_limit(est)),
        name="mixer",
    )(x, yf, yb, shift, scale, gate, g1, w_rest, gv, ws, bs, wpr, wpg, wout)


def _ffn_kernel(xm_ref, xp_ref, xn_ref, sh_ref, sc_ref, gt_ref, g2_ref, wg_ref, wv_ref, cw_ref,
                cb_ref, wd_ref, gf_ref, o_ref, h_s, act_s, *, tm, n_tiles):
    i = pl.program_id(1)
    rows = tm + 2 * GRID_W
    g2 = g2_ref[...]
    sc = 1.0 + sc_ref[...]
    sh = sh_ref[...]

    def modulated(x):
        return _rms(x, g2) * sc + sh

    keep_prev = jnp.where(i > 0, 1.0, 0.0)
    keep_next = jnp.where(i < n_tiles - 1, 1.0, 0.0)
    h_s[0:GRID_W, :] = (modulated(xp_ref[...]) * keep_prev).astype(BF16)
    h_s[GRID_W:GRID_W + tm, :] = modulated(xm_ref[...]).astype(BF16)
    h_s[GRID_W + tm:rows, :] = (modulated(xn_ref[...]) * keep_next).astype(BF16)

    col = lax.broadcasted_iota(jnp.int32, (rows, FFN_COLS), 0) & (GRID_W - 1)
    has_left = jnp.where(col == 0, 0.0, 1.0).astype(BF16)
    has_right = jnp.where(col == GRID_W - 1, 0.0, 1.0).astype(BF16)

    for c in range(D_FF // FFN_COLS):
        g = jnp.dot(h_s[...], wg_ref[c], preferred_element_type=F32)
        g_c = g.astype(BF16)
        g_l = pltpu.roll(g, 1, 0).astype(BF16) * has_left
        g_r = pltpu.roll(g, rows - 1, 0).astype(BF16) * has_right
        cw = cw_ref[c]
        conv = cb_ref[c]
        for dr in range(3):
            lo = GRID_W * dr
            conv = (conv + cw[3 * dr:3 * dr + 1, :] * g_l[lo:lo + tm]
                    + cw[3 * dr + 1:3 * dr + 2, :] * g_c[lo:lo + tm]
                    + cw[3 * dr + 2:3 * dr + 3, :] * g_r[lo:lo + tm])
        val = jnp.dot(h_s[GRID_W:GRID_W + tm, :], wv_ref[c], preferred_element_type=F32)
        act_s[:, c * FFN_COLS:(c + 1) * FFN_COLS] = (_gelu(conv.astype(F32)) * val).astype(BF16)
    down = jnp.dot(act_s[...], wd_ref[...], preferred_element_type=F32)
    o_ref[...] = _rms(xm_ref[...] + gt_ref[...] * down, gf_ref[...])


def _ffn_call(x1, shift, scale, gate, g2, wg3, wv3, cw3, cb3, wd3, gf):
    nb, t, d = x1.shape
    tm = FFN_ROWS
    n_tiles = t // tm
    r_h = tm // GRID_W
    n_h = t // GRID_W
    nc = D_FF // FFN_COLS
    vec = pl.BlockSpec((None, 1, d), lambda b, i: (b, 0, 0))
    est = (2 * 2 * tm * d * 4 + 4 * GRID_W * d * 4 + (2 * d * D_FF + D_FF * d) * 2
           + (tm + 2 * GRID_W) * d * 2 + tm * d * 4 + 8 * (tm + 2 * GRID_W) * FFN_COLS * 4)
    return pl.pallas_call(
        functools.partial(_ffn_kernel, tm=tm, n_tiles=n_tiles),
        grid=(nb, n_tiles),
        in_specs=[pl.BlockSpec((None, tm, d), lambda b, i: (b, i, 0)),
                  pl.BlockSpec((None, GRID_W, d), lambda b, i: (b, jnp.maximum(i * r_h - 1, 0), 0)),
                  pl.BlockSpec((None, GRID_W, d),
                               lambda b, i: (b, jnp.minimum((i + 1) * r_h, n_h - 1), 0)),
                  vec, vec, vec, _const_spec((1, d)),
                  _const_spec((nc, d, FFN_COLS)), _const_spec((nc, d, FFN_COLS)),
                  _const_spec((nc, 9, FFN_COLS)), _const_spec((nc, 1, FFN_COLS)),
                  _const_spec((D_FF, d)), _const_spec((1, d))],
        out_specs=pl.BlockSpec((None, tm, d), lambda b, i: (b, i, 0)),
        out_shape=jax.ShapeDtypeStruct((nb, t, d), F32),
        scratch_shapes=[pltpu.VMEM((tm + 2 * GRID_W, d), BF16), pltpu.VMEM((tm, D_FF), BF16)],
        compiler_params=pltpu.CompilerParams(
            dimension_semantics=("arbitrary", "arbitrary"), vmem_limit_bytes=_vmem_limit(est)),
        name="ffn",
    )(x1, x1, x1, shift, scale, gate, g2, wg3, wv3, cw3, cb3, wd3, gf)


def kernel(x, c, ctx, c_ctx, w_mod, b_mod, g_norm1, w_in, conv_w, conv_b, lru_lam, lru_wa, lru_ba,
           lru_wx, lru_bx, g_v, w_s, b_s, w_pr, w_pg, w_out, g_norm2, w_up, ffn_conv_w, ffn_conv_b,
           w_down, g_final):
    nb, t, d = x.shape
    depth = w_in.shape[0]
    assert depth == 1 and nb == V7X_SUBLANES and d == D_MODEL
    assert t % max(LRU_STEPS, MIX_ROWS, FFN_ROWS) == 0 and ctx.shape[1] % LRU_STEPS == 0
    l = 0

    pad = jnp.zeros((2 * V7X_SUBLANES - nb - 1, d), F32)
    cc = jnp.concatenate([c, c_ctx[None, :], pad], axis=0)
    m = _mod_call(cc, w_mod[l], b_mod[l][None, :]).reshape(cc.shape[0], N_MOD, d)
    m_x = m[:nb]
    m_c = jnp.broadcast_to(m[nb][None], (nb, N_MOD, d))

    w_in_b = w_in[l].astype(BF16)
    w_r, w_rest = w_in_b[:, :W_R], w_in_b[:, W_R:]
    g1 = g_norm1[l][None, :]
    wg = (0.5 * jnp.concatenate([lru_wa[l], lru_wx[l]], axis=-1)).astype(BF16)
    bg = 0.5 * jnp.stack([lru_ba[l][0], lru_bx[l][0], lru_ba[l][1], lru_bx[l][1]], axis=0)
    conv_args = (conv_w[l], conv_b[l][None, :])
    lru_args = (lru_lam[l], wg, bg)

    pc = _inproj_call(ctx, m_c[:, 0], m_c[:, 1], g1, w_r, *conv_args)
    _, _, h_ctx = _lru_call(pc, nb, *lru_args, jnp.zeros((2, nb, W_R), F32))

    px = _inproj_call(x, m_x[:, 0], m_x[:, 1], g1, w_r, *conv_args)
    yf, yb, _ = _lru_call(px, nb, *lru_args, h_ctx)

    bs_e = jnp.broadcast_to(b_s[l].T[:, :, None], (H_G, CHUNK, GC))
    x1 = _mixer_call(x, yf, yb, m_x[:, 0:1], m_x[:, 1:2], m_x[:, 2:3], g1, w_rest,
                     g_v[l][None, :], w_s[l].astype(BF16), bs_e,
                     w_pr[l].astype(BF16), w_pg[l].astype(BF16), w_out[l].astype(BF16))

    nc = D_FF // FFN_COLS
    w_up_b = w_up[l].astype(BF16)
    wg3 = w_up_b[:, :D_FF].reshape(d, nc, FFN_COLS).transpose(1, 0, 2)
    wv3 = w_up_b[:, D_FF:].reshape(d, nc, FFN_COLS).transpose(1, 0, 2)
    cw3 = ffn_conv_w[l].reshape(9, nc, FFN_COLS).transpose(1, 0, 2).astype(BF16)
    cb3 = ffn_conv_b[l].reshape(nc, 1, FFN_COLS).astype(BF16)
    wd3 = w_down[l].astype(BF16)
    return _ffn_call(x1, m_x[:, 3:4], m_x[:, 4:5], m_x[:, 5:6], g_norm2[l][None, :],
                     wg3, wv3, cw3, cb3, wd3, g_final[None, :])
```

```python
import functools
import math

import jax
import jax.numpy as jnp
from jax import lax
from jax.experimental import pallas as pl
from jax.experimental.pallas import tpu as pltpu

D_MODEL = 1024
GRID_W = 64
W_R = 1280
H_R = 5
BW = W_R // H_R
LRU_C = 8.0
CONV_R = 4
W_G = 1024
H_G = 8
GC = W_G // H_G
CHUNK = 128
D_FF = 2816
N_MOD = 6
EPS = 1e-6
RSQRT_FLOOR = 1e-30
OFF_RG = W_R
OFF_U = OFF_RG + W_R
OFF_V = OFF_U + W_G
OFF_GR = OFF_V + W_G
OFF_GG = OFF_GR + D_MODEL
N_IN = OFF_GG + D_MODEL

V7X_LANES = 128
V7X_SUBLANES = 8
V7X_MXU_WIDTH = 256
V7X_VMEM_BYTES = 64 * 1024 * 1024

F32 = jnp.float32
BF16 = jnp.bfloat16

LRU_STEPS = 64
MIX_ROWS = 512
FFN_ROWS = 512
FFN_COLS = V7X_MXU_WIDTH
MOD_COLS = 1024
LOOKAHEAD = 2


def _vmem_limit(nbytes):
    return int(min(max(2 * nbytes, 16 * 1024 * 1024), V7X_VMEM_BYTES - 8 * 1024 * 1024))


def _gelu(x):
    k = math.sqrt(2.0 / math.pi)
    hx = 0.5 * x
    return hx + hx * jnp.tanh(x * (k + (k * 0.044715) * (x * x)))


def _sigmoid(x):
    return 1.0 / (1.0 + jnp.exp2(x * (-math.log2(math.e))))


def _rms(x, g):
    return x * lax.rsqrt(jnp.mean(x * x, axis=-1, keepdims=True) + EPS) * g


def _const_spec(shape):
    nd = len(shape)
    return pl.BlockSpec(shape, lambda *_: (0,) * nd, pipeline_mode=pl.Buffered(1))


def _mod_kernel(c_ref, w_ref, b_ref, o_ref):
    c = c_ref[...]
    s = c * _sigmoid(c)
    o_ref[...] = jnp.dot(s, w_ref[...], preferred_element_type=F32,
                         precision=lax.Precision.HIGHEST) + b_ref[...]


def _mod_call(cc, w, b):
    rows, d = cc.shape
    n = w.shape[1]
    return pl.pallas_call(
        _mod_kernel,
        grid=(n // MOD_COLS,),
        in_specs=[pl.BlockSpec((rows, d), lambda j: (0, 0)),
                  pl.BlockSpec((d, MOD_COLS), lambda j: (0, j)),
                  pl.BlockSpec((1, MOD_COLS), lambda j: (0, j))],
        out_specs=pl.BlockSpec((rows, MOD_COLS), lambda j: (0, j)),
        out_shape=jax.ShapeDtypeStruct((rows, n), F32),
        compiler_params=pltpu.CompilerParams(
            dimension_semantics=("arbitrary",),
            vmem_limit_bytes=_vmem_limit(2 * d * MOD_COLS * 4)),
        name="mod",
    )(cc, w, b)


def _inproj_kernel(x_ref, xn_ref, sh_ref, sc_ref, g_ref, w_ref, cw_ref, cb_ref, o_ref, hs_ref, tail_ref,
                   *, steps, n_tiles):
    i = pl.program_id(0)
    nb = x_ref.shape[0]
    rows = steps * nb
    n_slab = D_MODEL // V7X_LANES
    g = g_ref[...]

    @pl.when(i == 0)
    def _():
        tail_ref[...] = jnp.zeros_like(tail_ref)

    half = steps // 2
    parts = []
    for part in range(2):
        t0 = part * half
        for b in range(nb):
            sc = 1.0 + sc_ref[b:b + 1, :]
            sh = sh_ref[b:b + 1, :]
            hb = _rms(x_ref[b, t0:t0 + half, :], g) * sc + sh
            for s in range(n_slab):
                lanes = slice(s * V7X_LANES, (s + 1) * V7X_LANES)
                hs_ref[s, pl.ds(t0 * nb + b, half, stride=nb), :] = hb[:, lanes]
            if part == 1:
                hn = _rms(xn_ref[b], g)[0:LOOKAHEAD] * sc + sh
                for s in range(n_slab):
                    lanes = slice(s * V7X_LANES, (s + 1) * V7X_LANES)
                    hs_ref[s, pl.ds(rows + b, LOOKAHEAD, stride=nb), :] = hn[:, lanes]
        lo = t0 * nb
        hi = lo + half * nb + (LOOKAHEAD * nb if part == 1 else 0)
        h = jnp.concatenate([hs_ref[s, lo:hi, :] for s in range(n_slab)], axis=1).astype(BF16)
        parts.append(jnp.dot(h, w_ref[...], preferred_element_type=F32))
    p = jnp.concatenate(parts, axis=0)

    keep_next = jnp.where(i < n_tiles - 1, 1.0, 0.0)
    pe = jnp.concatenate([tail_ref[...], p[0:rows], p[rows:rows + nb] * keep_next], axis=0)
    xr = cb_ref[...]
    for k in range(CONV_R):
        xr = xr + cw_ref[k:k + 1, :] * pe[k * nb:k * nb + rows]
    o_ref[...] = xr
    tail_ref[...] = p[rows - (CONV_R - 2) * nb:rows]


def _inproj_call(x, shift, scale, g, w_r, cw, cb):
    nb, t, d = x.shape
    steps = LRU_STEPS
    rows = steps * nb
    n_tiles = t // steps
    look = V7X_SUBLANES
    r_l = steps // look
    n_l = t // look
    est = (2 * nb * (steps + look) * d * 4 + d * W_R * 2 + 2 * rows * W_R * 4
           + (rows + LOOKAHEAD * nb) * (d + 2 * W_R) * 4)
    return pl.pallas_call(
        functools.partial(_inproj_kernel, steps=steps, n_tiles=n_tiles),
        grid=(n_tiles,),
        in_specs=[pl.BlockSpec((nb, steps, d), lambda i: (0, i, 0)),
                  pl.BlockSpec((nb, look, d), lambda i: (0, jnp.minimum((i + 1) * r_l, n_l - 1), 0)),
                  _const_spec((nb, d)), _const_spec((nb, d)), _const_spec((1, d)),
                  _const_spec((d, W_R)), _const_spec((CONV_R, W_R)), _const_spec((1, W_R))],
        out_specs=pl.BlockSpec((rows, W_R), lambda i: (i, 0)),
        out_shape=jax.ShapeDtypeStruct((t * nb, W_R), F32),
        scratch_shapes=[pltpu.VMEM((d // V7X_LANES, rows + LOOKAHEAD * nb, V7X_LANES), F32),
                        pltpu.VMEM(((CONV_R - 2) * nb, W_R), F32)],
        compiler_params=pltpu.CompilerParams(
            dimension_semantics=("arbitrary",), vmem_limit_bytes=_vmem_limit(est)),
        name="inproj",
    )(x, x, shift, scale, g, w_r, cw, cb)


def _lru_kernel(xf_ref, xb_ref, lam_ref, wg_ref, bg_ref, h0_ref,
                yf_ref, yb_ref, hl_ref, a_s, b_s, y_s, hc, *, steps):
    nb = V7X_SUBLANES
    n_slab = W_R // V7X_LANES

    @pl.when(pl.program_id(0) == 0)
    def _():
        hc[...] = h0_ref[...]

    z = -lam_ref[...]
    softplus = jnp.maximum(z, 0.0) + jnp.log1p(jnp.exp(-jnp.abs(z)))
    half_decay = (-0.5 * LRU_C * math.log2(math.e)) * softplus

    def coefficients(d, x_ref, h):
        cs = slice(h * BW, (h + 1) * BW)
        hx = x_ref[:, cs]
        half_gates = jnp.dot(hx.astype(BF16), wg_ref[d, h], preferred_element_type=F32)
        th_r = jnp.tanh(half_gates[:, :BW] + bg_ref[2 * d:2 * d + 1, cs])
        th_i = jnp.tanh(half_gates[:, BW:] + bg_ref[2 * d + 1:2 * d + 2, cs])
        a = jnp.exp2(half_decay[d:d + 1, cs] + th_r * half_decay[d:d + 1, cs])
        a_s[d, :, cs] = a
        y = 1.0 - a * a
        gated_x = hx * (1.0 + th_i)
        b_s[d, :, cs] = (y * lax.rsqrt(jnp.maximum(y, RSQRT_FLOOR))) * gated_x

    slabs_per_head = BW // V7X_LANES
    for h in range(H_R):
        cs = slice(h * BW, (h + 1) * BW)
        coefficients(0, xf_ref, h)
        coefficients(1, xb_ref, h)
        hf = hc[0, :, cs]
        hb = hc[1, :, cs]
        for t in range(steps):
            rf = slice(t * nb, (t + 1) * nb)
            rb = slice((steps - 1 - t) * nb, (steps - t) * nb)
            hf = a_s[0, rf, cs] * hf + b_s[0, rf, cs]
            hb = a_s[1, rb, cs] * hb + b_s[1, rb, cs]
            for k in range(slabs_per_head):
                lanes = slice(k * V7X_LANES, (k + 1) * V7X_LANES)
                y_s[0, h * slabs_per_head + k, rf, :] = hf[:, lanes]
                y_s[1, h * slabs_per_head + k, rb, :] = hb[:, lanes]
        hc[0, :, cs] = hf
        hc[1, :, cs] = hb
        hl_ref[0, :, cs] = hf
        hl_ref[1, :, cs] = hb

    for b in range(nb):
        yf_ref[b] = jnp.concatenate(
            [y_s[0, s, pl.ds(b, steps, stride=nb), :] for s in range(n_slab)], axis=1).astype(BF16)
        yb_ref[b] = jnp.concatenate(
            [y_s[1, s, pl.ds(b, steps, stride=nb), :] for s in range(n_slab)], axis=1).astype(BF16)


def _lru_call(xr, nb, lam, wg, bg, h0):
    steps = LRU_STEPS
    rows = steps * nb
    t = xr.shape[0] // nb
    n_tiles = t // steps
    est = (2 * 2 * rows * W_R * 4 + 2 * 2 * nb * steps * W_R * 2 + 3 * 2 * rows * W_R * 4
           + 2 * H_R * BW * 2 * BW * 2 + 4 * rows * W_R * 4)
    return pl.pallas_call(
        functools.partial(_lru_kernel, steps=steps),
        grid=(n_tiles,),
        in_specs=[pl.BlockSpec((rows, W_R), lambda i: (i, 0)),
                  pl.BlockSpec((rows, W_R), lambda i: (n_tiles - 1 - i, 0)),
                  _const_spec((2, W_R)), _const_spec((2, H_R, BW, 2 * BW)), _const_spec((4, W_R)),
                  _const_spec((2, nb, W_R))],
        out_specs=[pl.BlockSpec((nb, steps, W_R), lambda i: (0, i, 0)),
                   pl.BlockSpec((nb, steps, W_R), lambda i: (0, n_tiles - 1 - i, 0)),
                   pl.BlockSpec((2, nb, W_R), lambda i: (0, 0, 0))],
        out_shape=[jax.ShapeDtypeStruct((nb, t, W_R), BF16),
                   jax.ShapeDtypeStruct((nb, t, W_R), BF16),
                   jax.ShapeDtypeStruct((2, nb, W_R), F32)],
        scratch_shapes=[pltpu.VMEM((2, rows, W_R), F32),
                        pltpu.VMEM((2, rows, W_R), F32),
                        pltpu.VMEM((2, W_R // V7X_LANES, rows, V7X_LANES), F32),
                        pltpu.VMEM((2, nb, W_R), F32)],
        compiler_params=pltpu.CompilerParams(
            dimension_semantics=("arbitrary",), vmem_limit_bytes=_vmem_limit(est)),
        name="lru",
    )(xr, xr, lam, wg, bg, h0)


def _mixer_kernel(x_ref, yf_ref, yb_ref, sh_ref, sc_ref, gt_ref, g1_ref, win_ref, gv_ref,
                  ws_ref, bs_ref, wpr_ref, wpg_ref, wout_ref, o_ref, *, tm):
    w = V7X_MXU_WIDTH
    x = x_ref[...]
    h = (_rms(x, g1_ref[...]) * (1.0 + sc_ref[...]) + sh_ref[...]).astype(BF16)

    def cols(j):
        return slice(j * w, (j + 1) * w)

    def proj(off, j):
        lo = off + j * w
        return jnp.dot(h, win_ref[:, lo:lo + w], preferred_element_type=F32)

    v = jnp.concatenate([_gelu(proj(OFF_V, j)) for j in range(W_G // w)], axis=1)
    v = _rms(v, gv_ref[...]).astype(BF16)

    y_r = jnp.concatenate(
        [(_gelu(proj(OFF_RG, j)) * (yf_ref[:, cols(j)].astype(F32) + yb_ref[:, cols(j)].astype(F32))
          ).astype(BF16) for j in range(W_R // w)], axis=1)

    chunks = []
    for n in range(tm // CHUNK):
        heads = []
        for hh in range(H_G):
            blk = v[n * CHUNK:(n + 1) * CHUNK, hh * GC:(hh + 1) * GC]
            heads.append(jnp.dot(ws_ref[hh], blk, preferred_element_type=F32) + bs_ref[hh])
        chunks.append(jnp.concatenate(heads, axis=1))
    s = jnp.concatenate(chunks, axis=0)
    y_g = jnp.concatenate(
        [(_gelu(proj(OFF_U, j)) * s[:, cols(j)]).astype(BF16) for j in range(W_G // w)], axis=1)

    merged = []
    for j in range(D_MODEL // w):
        part_r = _sigmoid(proj(OFF_GR, j)) * jnp.dot(y_r, wpr_ref[:, cols(j)],
                                                     preferred_element_type=F32)
        part_g = _sigmoid(proj(OFF_GG, j)) * jnp.dot(y_g, wpg_ref[:, cols(j)],
                                                     preferred_element_type=F32)
        merged.append((part_r + part_g).astype(BF16))
    merged = jnp.concatenate(merged, axis=1)
    o_ref[...] = x + gt_ref[...] * jnp.dot(merged, wout_ref[...], preferred_element_type=F32)


def _mixer_call(x, yf, yb, shift, scale, gate, g1, w_in, gv, ws, bs, wpr, wpg, wout):
    nb, t, d = x.shape
    tm = MIX_ROWS
    tile = lambda width: pl.BlockSpec((None, tm, width), lambda b, i: (b, i, 0))
    vec = pl.BlockSpec((None, 1, d), lambda b, i: (b, 0, 0))
    est = (4 * tm * d * 4 + 4 * tm * W_R * 2
           + (d * N_IN + W_R * d + 2 * d * d + H_G * CHUNK * CHUNK) * 2 + H_G * CHUNK * GC * 4
           + 6 * tm * W_R * 4)
    return pl.pallas_call(
        functools.partial(_mixer_kernel, tm=tm),
        grid=(nb, t // tm),
        in_specs=[tile(d), tile(W_R), tile(W_R), vec, vec, vec,
                  _const_spec((1, d)), _const_spec((d, N_IN)), _const_spec((1, W_G)),
                  _const_spec((H_G, CHUNK, CHUNK)), _const_spec((H_G, CHUNK, GC)),
                  _const_spec((W_R, d)), _const_spec((W_G, d)), _const_spec((d, d))],
        out_specs=tile(d),
        out_shape=jax.ShapeDtypeStruct((nb, t, d), F32),
        compiler_params=pltpu.CompilerParams(
            dimension_semantics=("arbitrary", "arbitrary"), vmem_limit_bytes=_vmem_limit(est)),
        name="mixer",
    )(x, yf, yb, shift, scale, gate, g1, w_in, gv, ws, bs, wpr, wpg, wout)


def _ffn_kernel(xm_ref, xp_ref, xn_ref, sh_ref, sc_ref, gt_ref, g2_ref, wu_ref, cw_ref,
                cb_ref, wd_ref, gf_ref, o_ref, h_s, act_s, *, tm, n_tiles):
    i = pl.program_id(1)
    rows = tm + 2 * GRID_W
    g2 = g2_ref[...]
    sc = 1.0 + sc_ref[...]
    sh = sh_ref[...]

    def modulated(x):
        return _rms(x, g2) * sc + sh

    keep_prev = jnp.where(i > 0, 1.0, 0.0)
    keep_next = jnp.where(i < n_tiles - 1, 1.0, 0.0)
    h_s[0:GRID_W, :] = (modulated(xp_ref[...]) * keep_prev).astype(BF16)
    h_s[GRID_W:GRID_W + tm, :] = modulated(xm_ref[...]).astype(BF16)
    h_s[GRID_W + tm:rows, :] = (modulated(xn_ref[...]) * keep_next).astype(BF16)

    col = lax.broadcasted_iota(jnp.int32, (rows, FFN_COLS), 0) & (GRID_W - 1)
    has_left = jnp.where(col == 0, 0.0, 1.0).astype(BF16)
    has_right = jnp.where(col == GRID_W - 1, 0.0, 1.0).astype(BF16)

    for c in range(D_FF // FFN_COLS):
        gate_cols = slice(c * FFN_COLS, (c + 1) * FFN_COLS)
        val_cols = slice(D_FF + c * FFN_COLS, D_FF + (c + 1) * FFN_COLS)
        g = jnp.dot(h_s[...], wu_ref[:, gate_cols], preferred_element_type=F32)
        g_c = g.astype(BF16)
        g_l = pltpu.roll(g, 1, 0).astype(BF16) * has_left
        g_r = pltpu.roll(g, rows - 1, 0).astype(BF16) * has_right
        cw = cw_ref[c]
        conv = cb_ref[c]
        for dr in range(3):
            lo = GRID_W * dr
            conv = (conv + cw[3 * dr:3 * dr + 1, :] * g_l[lo:lo + tm]
                    + cw[3 * dr + 1:3 * dr + 2, :] * g_c[lo:lo + tm]
                    + cw[3 * dr + 2:3 * dr + 3, :] * g_r[lo:lo + tm])
        val = jnp.dot(h_s[GRID_W:GRID_W + tm, :], wu_ref[:, val_cols], preferred_element_type=F32)
        act_s[:, gate_cols] = (_gelu(conv.astype(F32)) * val).astype(BF16)
    down = jnp.dot(act_s[...], wd_ref[...], preferred_element_type=F32)
    o_ref[...] = _rms(xm_ref[...] + gt_ref[...] * down, gf_ref[...])


def _ffn_call(x1, shift, scale, gate, g2, w_up, cw3, cb3, w_down, gf):
    nb, t, d = x1.shape
    tm = FFN_ROWS
    n_tiles = t // tm
    r_h = tm // GRID_W
    n_h = t // GRID_W
    nc = D_FF // FFN_COLS
    vec = pl.BlockSpec((None, 1, d), lambda b, i: (b, 0, 0))
    est = (2 * 2 * tm * d * 4 + 4 * GRID_W * d * 4 + (2 * d * D_FF + D_FF * d) * 2
           + (tm + 2 * GRID_W) * d * 2 + tm * d * 4 + 8 * (tm + 2 * GRID_W) * FFN_COLS * 4)
    return pl.pallas_call(
        functools.partial(_ffn_kernel, tm=tm, n_tiles=n_tiles),
        grid=(nb, n_tiles),
        in_specs=[pl.BlockSpec((None, tm, d), lambda b, i: (b, i, 0)),
                  pl.BlockSpec((None, GRID_W, d), lambda b, i: (b, jnp.maximum(i * r_h - 1, 0), 0)),
                  pl.BlockSpec((None, GRID_W, d),
                               lambda b, i: (b, jnp.minimum((i + 1) * r_h, n_h - 1), 0)),
                  vec, vec, vec, _const_spec((1, d)),
                  _const_spec((d, 2 * D_FF)),
                  _const_spec((nc, 9, FFN_COLS)), _const_spec((nc, 1, FFN_COLS)),
                  _const_spec((D_FF, d)), _const_spec((1, d))],
        out_specs=pl.BlockSpec((None, tm, d), lambda b, i: (b, i, 0)),
        out_shape=jax.ShapeDtypeStruct((nb, t, d), F32),
        scratch_shapes=[pltpu.VMEM((tm + 2 * GRID_W, d), BF16), pltpu.VMEM((tm, D_FF), BF16)],
        compiler_params=pltpu.CompilerParams(
            dimension_semantics=("arbitrary", "arbitrary"), vmem_limit_bytes=_vmem_limit(est)),
        name="ffn",
    )(x1, x1, x1, shift, scale, gate, g2, w_up, cw3, cb3, w_down, gf)


def kernel(x, c, ctx, c_ctx, w_mod, b_mod, g_norm1, w_in, conv_w, conv_b, lru_lam, lru_wa, lru_ba,
           lru_wx, lru_bx, g_v, w_s, b_s, w_pr, w_pg, w_out, g_norm2, w_up, ffn_conv_w, ffn_conv_b,
           w_down, g_final):
    nb, t, d = x.shape
    depth = w_in.shape[0]
    assert depth == 1 and nb == V7X_SUBLANES and d == D_MODEL
    assert t % max(LRU_STEPS, MIX_ROWS, FFN_ROWS) == 0 and ctx.shape[1] % LRU_STEPS == 0
    l = 0

    pad = jnp.zeros((2 * V7X_SUBLANES - nb - 1, d), F32)
    cc = jnp.concatenate([c, c_ctx[None, :], pad], axis=0)
    m = _mod_call(cc, w_mod[l], b_mod[l][None, :]).reshape(cc.shape[0], N_MOD, d)
    m_x = m[:nb]
    m_c = jnp.broadcast_to(m[nb][None], (nb, N_MOD, d))

    w_in_b = w_in[l].astype(BF16)
    w_r = w_in_b[:, :W_R]
    g1 = g_norm1[l][None, :]
    wg = jnp.concatenate([lru_wa[l], lru_wx[l]], axis=-1).astype(BF16)
    bg = 0.5 * jnp.stack([lru_ba[l][0], lru_bx[l][0], lru_ba[l][1], lru_bx[l][1]], axis=0)
    conv_args = (0.5 * conv_w[l], 0.5 * conv_b[l][None, :])
    lru_args = (lru_lam[l], wg, bg)

    pc = _inproj_call(ctx, m_c[:, 0], m_c[:, 1], g1, w_r, *conv_args)
    _, _, h_ctx = _lru_call(pc, nb, *lru_args, jnp.zeros((2, nb, W_R), F32))

    px = _inproj_call(x, m_x[:, 0], m_x[:, 1], g1, w_r, *conv_args)
    yf, yb, _ = _lru_call(px, nb, *lru_args, h_ctx)

    bs_e = jnp.broadcast_to(b_s[l].T[:, :, None], (H_G, CHUNK, GC))
    x1 = _mixer_call(x, yf, yb, m_x[:, 0:1], m_x[:, 1:2], m_x[:, 2:3], g1, w_in_b,
                     g_v[l][None, :], w_s[l].astype(BF16), bs_e,
                     w_pr[l].astype(BF16), w_pg[l].astype(BF16), w_out[l].astype(BF16))

    nc = D_FF // FFN_COLS
    cw3 = ffn_conv_w[l].reshape(9, nc, FFN_COLS).transpose(1, 0, 2).astype(BF16)
    cb3 = ffn_conv_b[l].reshape(nc, 1, FFN_COLS).astype(BF16)
    return _ffn_call(x1, m_x[:, 3:4], m_x[:, 4:5], m_x[:, 5:6], g_norm2[l][None, :],
                     w_up[l].astype(BF16), cw3, cb3, w_down[l].astype(BF16), g_final[None, :])
```

```python
import functools
import math

import jax
import jax.numpy as jnp
from jax import lax
from jax.experimental import pallas as pl
from jax.experimental.pallas import tpu as pltpu

D_MODEL = 1024
GRID_W = 64
W_R = 1280
H_R = 5
BW = W_R // H_R
LRU_C = 8.0
CONV_R = 4
W_G = 1024
H_G = 8
GC = W_G // H_G
CHUNK = 128
D_FF = 2816
N_MOD = 6
EPS = 1e-6
RSQRT_FLOOR = 1e-30
OFF_RG = W_R
OFF_U = OFF_RG + W_R
OFF_V = OFF_U + W_G
OFF_GR = OFF_V + W_G
OFF_GG = OFF_GR + D_MODEL
N_IN = OFF_GG + D_MODEL

V7X_LANES = 128
V7X_SUBLANES = 8
V7X_MXU_WIDTH = 256
V7X_VMEM_BYTES = 64 * 1024 * 1024

F32 = jnp.float32
BF16 = jnp.bfloat16

LRU_STEPS = 64
MIX_ROWS = 512
FFN_ROWS = 512
FFN_COLS = V7X_MXU_WIDTH
MOD_COLS = 1024
LOOKAHEAD = 2


def _vmem_limit(nbytes):
    return int(min(max(2 * nbytes, 16 * 1024 * 1024), V7X_VMEM_BYTES - 8 * 1024 * 1024))


def _gelu(x):
    k = math.sqrt(2.0 / math.pi)
    hx = 0.5 * x
    return hx + hx * jnp.tanh(x * (k + (k * 0.044715) * (x * x)))


def _sigmoid(x):
    return 1.0 / (1.0 + jnp.exp2(x * (-math.log2(math.e))))


def _rms(x, g):
    return x * lax.rsqrt(jnp.mean(x * x, axis=-1, keepdims=True) + EPS) * g


def _const_spec(shape):
    nd = len(shape)
    return pl.BlockSpec(shape, lambda *_: (0,) * nd, pipeline_mode=pl.Buffered(1))


def _mod_kernel(c_ref, w_ref, b_ref, o_ref):
    c = c_ref[...]
    s = c * _sigmoid(c)
    o_ref[...] = jnp.dot(s, w_ref[...], preferred_element_type=F32,
                         precision=lax.Precision.HIGHEST) + b_ref[...]


def _mod_call(cc, w, b):
    rows, d = cc.shape
    n = w.shape[1]
    return pl.pallas_call(
        _mod_kernel,
        grid=(n // MOD_COLS,),
        in_specs=[pl.BlockSpec((rows, d), lambda j: (0, 0)),
                  pl.BlockSpec((d, MOD_COLS), lambda j: (0, j)),
                  pl.BlockSpec((1, MOD_COLS), lambda j: (0, j))],
        out_specs=pl.BlockSpec((rows, MOD_COLS), lambda j: (0, j)),
        out_shape=jax.ShapeDtypeStruct((rows, n), F32),
        compiler_params=pltpu.CompilerParams(
            dimension_semantics=("arbitrary",),
            vmem_limit_bytes=_vmem_limit(2 * d * MOD_COLS * 4)),
        name="mod",
    )(cc, w, b)


def _inproj_kernel(x_ref, xn_ref, sh_ref, sc_ref, g_ref, w_ref, cw_ref, cb_ref, o_ref, hs_ref, tail_ref,
                   *, steps, n_tiles):
    i = pl.program_id(0)
    nb = x_ref.shape[0]
    rows = steps * nb
    n_slab = D_MODEL // V7X_LANES
    g = g_ref[...]

    @pl.when(i == 0)
    def _():
        tail_ref[...] = jnp.zeros_like(tail_ref)

    half = steps // 2
    parts = []
    for part in range(2):
        t0 = part * half
        for b in range(nb):
            sc = 1.0 + sc_ref[b:b + 1, :]
            sh = sh_ref[b:b + 1, :]
            hb = _rms(x_ref[b, t0:t0 + half, :], g) * sc + sh
            for s in range(n_slab):
                lanes = slice(s * V7X_LANES, (s + 1) * V7X_LANES)
                hs_ref[s, pl.ds(t0 * nb + b, half, stride=nb), :] = hb[:, lanes]
            if part == 1:
                hn = _rms(xn_ref[b], g)[0:LOOKAHEAD] * sc + sh
                for s in range(n_slab):
                    lanes = slice(s * V7X_LANES, (s + 1) * V7X_LANES)
                    hs_ref[s, pl.ds(rows + b, LOOKAHEAD, stride=nb), :] = hn[:, lanes]
        lo = t0 * nb
        hi = lo + half * nb + (LOOKAHEAD * nb if part == 1 else 0)
        h = jnp.concatenate([hs_ref[s, lo:hi, :] for s in range(n_slab)], axis=1).astype(BF16)
        parts.append(jnp.dot(h, w_ref[...], preferred_element_type=F32))
    p = jnp.concatenate(parts, axis=0)

    keep_next = jnp.where(i < n_tiles - 1, 1.0, 0.0)
    pe = jnp.concatenate([tail_ref[...], p[0:rows], p[rows:rows + nb] * keep_next], axis=0)
    xr = cb_ref[...]
    for k in range(CONV_R):
        xr = xr + cw_ref[k:k + 1, :] * pe[k * nb:k * nb + rows]
    o_ref[...] = xr
    tail_ref[...] = p[rows - (CONV_R - 2) * nb:rows]


def _inproj_call(x, shift, scale, g, w_r, cw, cb):
    nb, t, d = x.shape
    steps = LRU_STEPS
    rows = steps * nb
    n_tiles = t // steps
    look = V7X_SUBLANES
    r_l = steps // look
    n_l = t // look
    est = (2 * nb * (steps + look) * d * 4 + d * W_R * 2 + 2 * rows * W_R * 4
           + (rows + LOOKAHEAD * nb) * (d + 2 * W_R) * 4)
    return pl.pallas_call(
        functools.partial(_inproj_kernel, steps=steps, n_tiles=n_tiles),
        grid=(n_tiles,),
        in_specs=[pl.BlockSpec((nb, steps, d), lambda i: (0, i, 0)),
                  pl.BlockSpec((nb, look, d), lambda i: (0, jnp.minimum((i + 1) * r_l, n_l - 1), 0)),
                  _const_spec((nb, d)), _const_spec((nb, d)), _const_spec((1, d)),
                  _const_spec((d, W_R)), _const_spec((CONV_R, W_R)), _const_spec((1, W_R))],
        out_specs=pl.BlockSpec((rows, W_R), lambda i: (i, 0)),
        out_shape=jax.ShapeDtypeStruct((t * nb, W_R), F32),
        scratch_shapes=[pltpu.VMEM((d // V7X_LANES, rows + LOOKAHEAD * nb, V7X_LANES), F32),
                        pltpu.VMEM(((CONV_R - 2) * nb, W_R), F32)],
        compiler_params=pltpu.CompilerParams(
            dimension_semantics=("arbitrary",), vmem_limit_bytes=_vmem_limit(est)),
        name="inproj",
    )(x, x, shift, scale, g, w_r, cw, cb)


def _lru_kernel(xf_ref, xb_ref, lam_ref, wg_ref, bg_ref, h0_ref,
                yf_ref, yb_ref, hl_ref, a_s, b_s, y_s, hc, *, steps):
    nb = V7X_SUBLANES
    n_slab = W_R // V7X_LANES

    @pl.when(pl.program_id(0) == 0)
    def _():
        hc[...] = h0_ref[...]

    z = -lam_ref[...]
    softplus = jnp.maximum(z, 0.0) + jnp.log1p(jnp.exp(-jnp.abs(z)))
    half_decay = (-0.5 * LRU_C * math.log2(math.e)) * softplus

    def coefficients(d, x_ref, h):
        cs = slice(h * BW, (h + 1) * BW)
        hx = x_ref[:, cs]
        half_gates = jnp.dot(hx.astype(BF16), wg_ref[d, h], preferred_element_type=F32)
        th_r = jnp.tanh(half_gates[:, :BW] + bg_ref[2 * d:2 * d + 1, cs])
        th_i = jnp.tanh(half_gates[:, BW:] + bg_ref[2 * d + 1:2 * d + 2, cs])
        a = jnp.exp2(half_decay[d:d + 1, cs] + th_r * half_decay[d:d + 1, cs])
        a_s[d, :, cs] = a
        y = 1.0 - a * a
        gated_x = hx * (1.0 + th_i)
        b_s[d, :, cs] = (y * lax.rsqrt(jnp.maximum(y, RSQRT_FLOOR))) * gated_x

    slabs_per_head = BW // V7X_LANES
    for h in range(H_R):
        cs = slice(h * BW, (h + 1) * BW)
        coefficients(0, xf_ref, h)
        coefficients(1, xb_ref, h)
        hf = hc[0, :, cs]
        hb = hc[1, :, cs]
        for t in range(steps):
            rf = slice(t * nb, (t + 1) * nb)
            rb = slice((steps - 1 - t) * nb, (steps - t) * nb)
            hf = a_s[0, rf, cs] * hf + b_s[0, rf, cs]
            hb = a_s[1, rb, cs] * hb + b_s[1, rb, cs]
            for k in range(slabs_per_head):
                lanes = slice(k * V7X_LANES, (k + 1) * V7X_LANES)
                y_s[0, h * slabs_per_head + k, rf, :] = hf[:, lanes]
                y_s[1, h * slabs_per_head + k, rb, :] = hb[:, lanes]
        hc[0, :, cs] = hf
        hc[1, :, cs] = hb
        hl_ref[0, :, cs] = hf
        hl_ref[1, :, cs] = hb

    for b in range(nb):
        yf_ref[b] = jnp.concatenate(
            [y_s[0, s, pl.ds(b, steps, stride=nb), :] for s in range(n_slab)], axis=1).astype(BF16)
        yb_ref[b] = jnp.concatenate(
            [y_s[1, s, pl.ds(b, steps, stride=nb), :] for s in range(n_slab)], axis=1).astype(BF16)


def _lru_call(xr, nb, lam, wg, bg, h0):
    steps = LRU_STEPS
    rows = steps * nb
    t = xr.shape[0] // nb
    n_tiles = t // steps
    est = (2 * 2 * rows * W_R * 4 + 2 * 2 * nb * steps * W_R * 2 + 3 * 2 * rows * W_R * 4
           + 2 * H_R * BW * 2 * BW * 2 + 4 * rows * W_R * 4)
    return pl.pallas_call(
        functools.partial(_lru_kernel, steps=steps),
        grid=(n_tiles,),
        in_specs=[pl.BlockSpec((rows, W_R), lambda i: (i, 0)),
                  pl.BlockSpec((rows, W_R), lambda i: (n_tiles - 1 - i, 0)),
                  _const_spec((2, W_R)), _const_spec((2, H_R, BW, 2 * BW)), _const_spec((4, W_R)),
                  _const_spec((2, nb, W_R))],
        out_specs=[pl.BlockSpec((nb, steps, W_R), lambda i: (0, i, 0)),
                   pl.BlockSpec((nb, steps, W_R), lambda i: (0, n_tiles - 1 - i, 0)),
                   pl.BlockSpec((2, nb, W_R), lambda i: (0, 0, 0))],
        out_shape=[jax.ShapeDtypeStruct((nb, t, W_R), BF16),
                   jax.ShapeDtypeStruct((nb, t, W_R), BF16),
                   jax.ShapeDtypeStruct((2, nb, W_R), F32)],
        scratch_shapes=[pltpu.VMEM((2, rows, W_R), F32),
                        pltpu.VMEM((2, rows, W_R), F32),
                        pltpu.VMEM((2, W_R // V7X_LANES, rows, V7X_LANES), F32),
                        pltpu.VMEM((2, nb, W_R), F32)],
        compiler_params=pltpu.CompilerParams(
            dimension_semantics=("arbitrary",), vmem_limit_bytes=_vmem_limit(est)),
        name="lru",
    )(xr, xr, lam, wg, bg, h0)


def _mixer_kernel(x_ref, yf_ref, yb_ref, sh_ref, sc_ref, gt_ref, g1_ref, win_ref, gv_ref,
                  ws_ref, bs_ref, wpr_ref, wpg_ref, wout_ref, o_ref, *, tm):
    w = V7X_MXU_WIDTH
    x = x_ref[...]
    h = (_rms(x, g1_ref[...]) * (1.0 + sc_ref[...]) + sh_ref[...]).astype(BF16)

    def cols(j):
        return slice(j * w, (j + 1) * w)

    def proj(off, j):
        lo = off + j * w
        return jnp.dot(h, win_ref[:, lo:lo + w], preferred_element_type=F32)

    v = jnp.concatenate([_gelu(proj(OFF_V, j)) for j in range(W_G // w)], axis=1)
    v = _rms(v, gv_ref[...]).astype(BF16)

    y_r = jnp.concatenate(
        [(_gelu(proj(OFF_RG, j)) * (yf_ref[:, cols(j)].astype(F32) + yb_ref[:, cols(j)].astype(F32))
          ).astype(BF16) for j in range(W_R // w)], axis=1)

    chunks = []
    for n in range(tm // CHUNK):
        heads = []
        for hh in range(H_G):
            blk = v[n * CHUNK:(n + 1) * CHUNK, hh * GC:(hh + 1) * GC]
            heads.append(jnp.dot(ws_ref[hh], blk, preferred_element_type=F32) + bs_ref[hh])
        chunks.append(jnp.concatenate(heads, axis=1))
    s = jnp.concatenate(chunks, axis=0)
    y_g = jnp.concatenate(
        [(_gelu(proj(OFF_U, j)) * s[:, cols(j)]).astype(BF16) for j in range(W_G // w)], axis=1)

    merged = []
    for j in range(D_MODEL // w):
        part_r = _sigmoid(proj(OFF_GR, j)) * jnp.dot(y_r, wpr_ref[:, cols(j)],
                                                     preferred_element_type=F32)
        part_g = _sigmoid(proj(OFF_GG, j)) * jnp.dot(y_g, wpg_ref[:, cols(j)],
                                                     preferred_element_type=F32)
        merged.append((part_r + part_g).astype(BF16))
    merged = jnp.concatenate(merged, axis=1)
    o_ref[...] = x + gt_ref[...] * jnp.dot(merged, wout_ref[...], preferred_element_type=F32)


def _mixer_call(x, yf, yb, shift, scale, gate, g1, w_in, gv, ws, bs, wpr, wpg, wout):
    nb, t, d = x.shape
    tm = MIX_ROWS
    tile = lambda width: pl.BlockSpec((None, tm, width), lambda b, i: (b, i, 0))
    vec = pl.BlockSpec((None, 1, d), lambda b, i: (b, 0, 0))
    est = (4 * tm * d * 4 + 4 * tm * W_R * 2
           + (d * N_IN + W_R * d + 2 * d * d + H_G * CHUNK * CHUNK) * 2 + H_G * CHUNK * GC * 4
           + 6 * tm * W_R * 4)
    return pl.pallas_call(
        functools.partial(_mixer_kernel, tm=tm),
        grid=(nb, t // tm),
        in_specs=[tile(d), tile(W_R), tile(W_R), vec, vec, vec,
                  _const_spec((1, d)), _const_spec((d, N_IN)), _const_spec((1, W_G)),
                  _const_spec((H_G, CHUNK, CHUNK)), _const_spec((H_G, CHUNK, GC)),
                  _const_spec((W_R, d)), _const_spec((W_G, d)), _const_spec((d, d))],
        out_specs=tile(d),
        out_shape=jax.ShapeDtypeStruct((nb, t, d), F32),
        compiler_params=pltpu.CompilerParams(
            dimension_semantics=("arbitrary", "arbitrary"), vmem_limit_bytes=_vmem_limit(est)),
        name="mixer",
    )(x, yf, yb, shift, scale, gate, g1, w_in, gv, ws, bs, wpr, wpg, wout)


def _ffn_kernel(xm_ref, xn_ref, sh_ref, sc_ref, gt_ref, g2_ref, wu_ref, cw_ref,
                cb_ref, wd_ref, gf_ref, o_ref, h_s, act_s, top_s, *, tm, n_tiles):
    i = pl.program_id(1)
    rows = tm + GRID_W
    rd = lax.rem(i, 2)
    wr = 1 - rd
    g2 = g2_ref[...]
    sc = 1.0 + sc_ref[...]
    sh = sh_ref[...]

    def modulated(x):
        return _rms(x, g2) * sc + sh

    @pl.when(i == 0)
    def _():
        top_s[...] = jnp.zeros_like(top_s)

    keep_next = jnp.where(i < n_tiles - 1, 1.0, 0.0)
    h_s[0:tm, :] = modulated(xm_ref[...]).astype(BF16)
    h_s[tm:rows, :] = (modulated(xn_ref[...]) * keep_next).astype(BF16)

    col = lax.broadcasted_iota(jnp.int32, (rows, FFN_COLS), 0) & (GRID_W - 1)
    has_left = jnp.where(col == 0, 0.0, 1.0).astype(BF16)
    has_right = jnp.where(col == GRID_W - 1, 0.0, 1.0).astype(BF16)

    for c in range(D_FF // FFN_COLS):
        gate_cols = slice(c * FFN_COLS, (c + 1) * FFN_COLS)
        val_cols = slice(D_FF + c * FFN_COLS, D_FF + (c + 1) * FFN_COLS)
        g = jnp.dot(h_s[...], wu_ref[:, gate_cols], preferred_element_type=F32)
        taps = (pltpu.roll(g, 1, 0).astype(BF16) * has_left,
                g.astype(BF16),
                pltpu.roll(g, rows - 1, 0).astype(BF16) * has_right)
        cw = cw_ref[c]
        conv = cb_ref[c]
        for k in range(3):
            ext = jnp.concatenate([top_s[rd, c, k], taps[k]], axis=0)
            for dr in range(3):
                lo = GRID_W * dr
                conv = conv + cw[3 * dr + k:3 * dr + k + 1, :] * ext[lo:lo + tm]
            top_s[wr, c, k] = taps[k][tm - GRID_W:tm]
        val = jnp.dot(h_s[0:tm, :], wu_ref[:, val_cols], preferred_element_type=F32)
        act_s[:, gate_cols] = (_gelu(conv.astype(F32)) * val).astype(BF16)
    down = jnp.dot(act_s[...], wd_ref[...], preferred_element_type=F32)
    o_ref[...] = _rms(xm_ref[...] + gt_ref[...] * down, gf_ref[...])


def _ffn_call(x1, shift, scale, gate, g2, w_up, cw3, cb3, w_down, gf):
    nb, t, d = x1.shape
    tm = FFN_ROWS
    n_tiles = t // tm
    r_h = tm // GRID_W
    n_h = t // GRID_W
    nc = D_FF // FFN_COLS
    vec = pl.BlockSpec((None, 1, d), lambda b, i: (b, 0, 0))
    est = (2 * 2 * tm * d * 4 + 4 * GRID_W * d * 4 + (2 * d * D_FF + D_FF * d) * 2
           + (tm + 2 * GRID_W) * d * 2 + tm * d * 4 + 8 * (tm + 2 * GRID_W) * FFN_COLS * 4)
    return pl.pallas_call(
        functools.partial(_ffn_kernel, tm=tm, n_tiles=n_tiles),
        grid=(nb, n_tiles),
        in_specs=[pl.BlockSpec((None, tm, d), lambda b, i: (b, i, 0)),
                  pl.BlockSpec((None, GRID_W, d),
                               lambda b, i: (b, jnp.minimum((i + 1) * r_h, n_h - 1), 0)),
                  vec, vec, vec, _const_spec((1, d)),
                  _const_spec((d, 2 * D_FF)),
                  _const_spec((nc, 9, FFN_COLS)), _const_spec((nc, 1, FFN_COLS)),
                  _const_spec((D_FF, d)), _const_spec((1, d))],
        out_specs=pl.BlockSpec((None, tm, d), lambda b, i: (b, i, 0)),
        out_shape=jax.ShapeDtypeStruct((nb, t, d), F32),
        scratch_shapes=[pltpu.VMEM((tm + GRID_W, d), BF16), pltpu.VMEM((tm, D_FF), BF16),
                        pltpu.VMEM((2, nc, 3, GRID_W, FFN_COLS), BF16)],
        compiler_params=pltpu.CompilerParams(
            dimension_semantics=("arbitrary", "arbitrary"), vmem_limit_bytes=_vmem_limit(est)),
        name="ffn",
    )(x1, x1, shift, scale, gate, g2, w_up, cw3, cb3, w_down, gf)


def kernel(x, c, ctx, c_ctx, w_mod, b_mod, g_norm1, w_in, conv_w, conv_b, lru_lam, lru_wa, lru_ba,
           lru_wx, lru_bx, g_v, w_s, b_s, w_pr, w_pg, w_out, g_norm2, w_up, ffn_conv_w, ffn_conv_b,
           w_down, g_final):
    nb, t, d = x.shape
    depth = w_in.shape[0]
    assert depth == 1 and nb == V7X_SUBLANES and d == D_MODEL
    assert t % max(LRU_STEPS, MIX_ROWS, FFN_ROWS) == 0 and ctx.shape[1] % LRU_STEPS == 0
    l = 0

    pad = jnp.zeros((2 * V7X_SUBLANES - nb - 1, d), F32)
    cc = jnp.concatenate([c, c_ctx[None, :], pad], axis=0)
    m = _mod_call(cc, w_mod[l], b_mod[l][None, :]).reshape(cc.shape[0], N_MOD, d)
    m_x = m[:nb]
    m_c = jnp.broadcast_to(m[nb][None], (nb, N_MOD, d))

    w_in_b = w_in[l].astype(BF16)
    w_r = w_in_b[:, :W_R]
    g1 = g_norm1[l][None, :]
    wg = jnp.concatenate([lru_wa[l], lru_wx[l]], axis=-1).astype(BF16)
    bg = 0.5 * jnp.stack([lru_ba[l][0], lru_bx[l][0], lru_ba[l][1], lru_bx[l][1]], axis=0)
    conv_args = (0.5 * conv_w[l], 0.5 * conv_b[l][None, :])
    lru_args = (lru_lam[l], wg, bg)

    pc = _inproj_call(ctx, m_c[:, 0], m_c[:, 1], g1, w_r, *conv_args)
    _, _, h_ctx = _lru_call(pc, nb, *lru_args, jnp.zeros((2, nb, W_R), F32))

    px = _inproj_call(x, m_x[:, 0], m_x[:, 1], g1, w_r, *conv_args)
    yf, yb, _ = _lru_call(px, nb, *lru_args, h_ctx)

    bs_e = jnp.broadcast_to(b_s[l].T[:, :, None], (H_G, CHUNK, GC))
    x1 = _mixer_call(x, yf, yb, m_x[:, 0:1], m_x[:, 1:2], m_x[:, 2:3], g1, w_in_b,
                     g_v[l][None, :], w_s[l].astype(BF16), bs_e,
                     w_pr[l].astype(BF16), w_pg[l].astype(BF16), w_out[l].astype(BF16))

    nc = D_FF // FFN_COLS
    cw3 = ffn_conv_w[l].reshape(9, nc, FFN_COLS).transpose(1, 0, 2).astype(BF16)
    cb3 = ffn_conv_b[l].reshape(nc, 1, FFN_COLS).astype(BF16)
    return _ffn_call(x1, m_x[:, 3:4], m_x[:, 4:5], m_x[:, 5:6], g_norm2[l][None, :],
                     w_up[l].astype(BF16), cw3, cb3, w_down[l].astype(BF16), g_final[None, :])
```
